```python
import jax, jax.numpy as jnp
from jax import lax
import numpy as np

D_MODEL = 2048
BATCH = 8
SEQ = 2048
DEPTH = 2

D_CONV = D_MODEL
K_CONV_A = 31
EXPAND = 2
D_INNER = EXPAND * D_MODEL
HEAD_DIM = 64
N_HEADS_SSM = D_INNER // HEAD_DIM
N_GROUPS = 8
HEADS_PER_GROUP = N_HEADS_SSM // N_GROUPS
D_STATE = 128
K_CONV_B = 4
CONV_B_DIM = D_INNER + 2 * N_GROUPS * D_STATE
CHUNK = 128
D_FF = ((8 * D_MODEL // 3 + 255) // 256) * 256
SPLIT_SIZES = (2 * D_CONV, D_INNER, CONV_B_DIM, N_HEADS_SSM, 2 * D_MODEL)
IN_COLS = sum(SPLIT_SIZES)
SPLIT_POINTS = tuple(int(v) for v in np.cumsum(SPLIT_SIZES)[:-1])
EPS = 1e-6

kernel_name = "gated_parallel_conformer_ssd_hybrid"


def rmsnorm(x, g):
    x32 = x.astype(jnp.float32)
    y = x32 * lax.rsqrt(jnp.mean(x32 * x32, axis=-1, keepdims=True) + EPS)
    return (y * g.astype(jnp.float32)).astype(x.dtype)


def layernorm(x, g, b):
    x32 = x.astype(jnp.float32)
    mu = jnp.mean(x32, axis=-1, keepdims=True)
    xc = x32 - mu
    var = jnp.mean(xc * xc, axis=-1, keepdims=True)
    y = xc * lax.rsqrt(var + EPS) * g.astype(jnp.float32) + b.astype(jnp.float32)
    return y.astype(x.dtype)


def causal_dwconv(u, w, b):
    k, c = w.shape
    out = lax.conv_general_dilated(
        u, w[:, None, :].astype(u.dtype), window_strides=(1,), padding=[(k - 1, 0)],
        dimension_numbers=("NWC", "WIO", "NWC"), feature_group_count=c)
    return out + b.astype(u.dtype)


def gated_group_rmsnorm(y, z, g):
    b, t, _ = y.shape
    v = (y * jax.nn.silu(z)).astype(jnp.float32).reshape(b, t, N_GROUPS, D_INNER // N_GROUPS)
    v = v * lax.rsqrt(jnp.mean(v * v, axis=-1, keepdims=True) + EPS)
    return (v.reshape(b, t, D_INNER) * g.astype(jnp.float32)).astype(y.dtype)


def ssd_chunked(xs, dt, a, bm, cm):
    b, t = xs.shape[:2]
    nc = t // CHUNK
    x_dt = (xs * dt[..., None]).reshape(b, nc, CHUNK, N_GROUPS, HEADS_PER_GROUP, HEAD_DIM)
    adt = (dt * a).reshape(b, nc, CHUNK, N_GROUPS, HEADS_PER_GROUP)
    adt = jnp.transpose(adt, (0, 3, 4, 1, 2))
    bc = bm.reshape(b, nc, CHUNK, N_GROUPS, D_STATE)
    cc = cm.reshape(b, nc, CHUNK, N_GROUPS, D_STATE)
    a_cs = jnp.cumsum(adt, axis=-1)
    causal = jnp.tril(jnp.ones((CHUNK, CHUNK), dtype=bool))
    seg = a_cs[..., :, None] - a_cs[..., None, :]
    lmat = jnp.exp(jnp.where(causal, seg, -jnp.inf))
    cb = jnp.einsum("bclgn,bcsgn->bcgls", cc, bc)
    y_diag = jnp.einsum("bcgls,bgrcls,bcsgrp->bclgrp", cb, lmat, x_dt)
    decay_states = jnp.exp(a_cs[..., -1:] - a_cs)
    states = jnp.einsum("bclgn,bgrcl,bclgrp->bcgrpn", bc, decay_states, x_dt)
    chunk_decay = jnp.exp(a_cs[..., -1])

    def step(carry, inp):
        st, dec = inp
        return carry * dec[..., None, None] + st, carry

    init = jnp.zeros((b, N_GROUPS, HEADS_PER_GROUP, HEAD_DIM, D_STATE), x_dt.dtype)
    _, prev = lax.scan(step, init, (jnp.moveaxis(states, 1, 0), jnp.moveaxis(chunk_decay, -1, 0)))
    prev = jnp.moveaxis(prev, 0, 1)
    y_off = jnp.einsum("bclgn,bgrcl,bcgrpn->bclgrp", cc, jnp.exp(a_cs), prev)
    return (y_diag + y_off).reshape(b, t, N_HEADS_SSM, HEAD_DIM)


def mixer_layer(h, w_in, conv_a_w, conv_a_b, ln_a_g, ln_a_b, w_conv_out,
                conv_b_w, conv_b_b, dt_bias, a_log, d_skip, ssm_norm_g, w_ssm_out, w_o):
    b, t, _ = h.shape
    proj = h @ w_in
    glu_in, z, xbc, dt_raw, gates = jnp.split(proj, SPLIT_POINTS, axis=-1)
    ua, ub = jnp.split(glu_in, 2, axis=-1)
    u = ua * jax.nn.sigmoid(ub)
    u = causal_dwconv(u, conv_a_w, conv_a_b)
    u = jax.nn.silu(layernorm(u, ln_a_g, ln_a_b))
    y_a = u @ w_conv_out
    xbc = jax.nn.silu(causal_dwconv(xbc, conv_b_w, conv_b_b))
    xs, bm, cm = jnp.split(xbc, (D_INNER, D_INNER + N_GROUPS * D_STATE), axis=-1)
    xs32 = xs.astype(jnp.float32).reshape(b, t, N_HEADS_SSM, HEAD_DIM)
    dt = jax.nn.softplus(dt_raw.astype(jnp.float32) + dt_bias.astype(jnp.float32))
    a = -jnp.exp(a_log.astype(jnp.float32))
    y = ssd_chunked(xs32, dt, a,
                    bm.astype(jnp.float32).reshape(b, t, N_GROUPS, D_STATE),
                    cm.astype(jnp.float32).reshape(b, t, N_GROUPS, D_STATE))
    y = y + d_skip.astype(jnp.float32)[:, None] * xs32
    y = y.reshape(b, t, D_INNER).astype(h.dtype)
    y_b = gated_group_rmsnorm(y, z, ssm_norm_g) @ w_ssm_out
    g = jax.nn.sigmoid(gates)
    g_a, g_b = jnp.split(g, 2, axis=-1)
    return (g_a * y_a + g_b * y_b) @ w_o


def swiglu(h, w_gate, w_up, w_down):
    return (jax.nn.silu(h @ w_gate) * (h @ w_up)) @ w_down


def setup_inputs(seed: int = 0) -> dict:
    key = jax.random.key(seed)
    ks = jax.random.split(key, 24)
    f32 = jnp.float32

    def nrm(k, shape, fan_in):
        return jax.random.normal(k, shape, f32) * (fan_in ** -0.5)

    def gain(k, shape):
        return 1.0 + 0.02 * jax.random.normal(k, shape, f32)

    def small(k, shape):
        return 0.02 * jax.random.normal(k, shape, f32)

    dt0 = jnp.exp(jax.random.uniform(ks[11], (DEPTH, N_HEADS_SSM), f32,
                                     jnp.log(1e-3), jnp.log(1e-1)))
    return {
        "x": jax.random.normal(ks[0], (BATCH, SEQ, D_MODEL), f32),
        "norm_mix_g": gain(ks[1], (DEPTH, D_MODEL)),
        "w_in": nrm(ks[2], (DEPTH, D_MODEL, IN_COLS), D_MODEL),
        "conv_a_w": nrm(ks[3], (DEPTH, K_CONV_A, D_CONV), K_CONV_A),
        "conv_a_b": small(ks[4], (DEPTH, D_CONV)),
        "ln_a_g": gain(ks[5], (DEPTH, D_CONV)),
        "ln_a_b": small(ks[6], (DEPTH, D_CONV)),
        "w_conv_out": nrm(ks[7], (DEPTH, D_CONV, D_MODEL), D_CONV),
        "conv_b_w": nrm(ks[8], (DEPTH, K_CONV_B, CONV_B_DIM), K_CONV_B),
        "conv_b_b": small(ks[9], (DEPTH, CONV_B_DIM)),
        "dt_bias": dt0 + jnp.log(-jnp.expm1(-dt0)),
        "a_log": jnp.log(jax.random.uniform(ks[10], (DEPTH, N_HEADS_SSM), f32, 1.0, 16.0)),
        "d_skip": gain(ks[12], (DEPTH, N_HEADS_SSM)),
        "ssm_norm_g": gain(ks[13], (DEPTH, D_INNER)),
        "w_ssm_out": nrm(ks[14], (DEPTH, D_INNER, D_MODEL), D_INNER),
        "w_o": nrm(ks[15], (DEPTH, D_MODEL, D_MODEL), D_MODEL),
        "norm_ffn_g": gain(ks[16], (DEPTH, D_MODEL)),
        "w_gate": nrm(ks[17], (DEPTH, D_MODEL, D_FF), D_MODEL),
        "w_up": nrm(ks[18], (DEPTH, D_MODEL, D_FF), D_MODEL),
        "w_down": nrm(ks[19], (DEPTH, D_FF, D_MODEL), D_FF),
        "final_g": gain(ks[20], (D_MODEL,)),
    }


def reference(x, norm_mix_g, w_in, conv_a_w, conv_a_b, ln_a_g, ln_a_b, w_conv_out,
              conv_b_w, conv_b_b, dt_bias, a_log, d_skip, ssm_norm_g, w_ssm_out, w_o,
              norm_ffn_g, w_gate, w_up, w_down, final_g):
    for l in range(DEPTH):
        h = rmsnorm(x, norm_mix_g[l])
        x = x + mixer_layer(h, w_in[l], conv_a_w[l], conv_a_b[l], ln_a_g[l], ln_a_b[l],
                            w_conv_out[l], conv_b_w[l], conv_b_b[l], dt_bias[l], a_log[l],
                            d_skip[l], ssm_norm_g[l], w_ssm_out[l], w_o[l])
        h = rmsnorm(x, norm_ffn_g[l])
        x = x + swiglu(h, w_gate[l], w_up[l], w_down[l])
    return rmsnorm(x, final_g)
```

```python
import functools

import jax
import jax.numpy as jnp
from jax import lax
from jax.experimental import pallas as pl
from jax.experimental.pallas import tpu as pltpu

F32 = jnp.float32
BF16 = jnp.bfloat16

D_MODEL = 2048
K_CONV_A = 31
D_INNER = 2 * D_MODEL
HEAD_DIM = 64
N_HEADS = D_INNER // HEAD_DIM
N_GROUPS = 8
HEADS_PER_GROUP = N_HEADS // N_GROUPS
D_STATE = 128
K_CONV_B = 4
BC_DIM = N_GROUPS * D_STATE
CONV_B_DIM = D_INNER + 2 * BC_DIM
CHUNK = 128
D_FF = 5632
EPS = 1e-6

COL_UA = 0
COL_UB = D_MODEL
COL_Z = 2 * D_MODEL
COL_XBC = 2 * D_MODEL + D_INNER
COL_GA = COL_XBC + CONV_B_DIM
COL_GB = COL_GA + D_MODEL
PROJ_COLS = COL_GB + D_MODEL

LANES = 128
SUBLANES = 8
HEAD_PAD = LANES
VMEM_LIMIT = 56 * 1024 * 1024


def _params(sem):
    return pltpu.CompilerParams(dimension_semantics=sem, vmem_limit_bytes=VMEM_LIMIT)


def _rows_loop(n_rows, blk, fn):
    def body(i, carry):
        fn(pl.multiple_of(i * blk, blk))
        return carry
    lax.fori_loop(0, n_rows // blk, body, 0)


def _rmsnorm_rows(x_ref, g_ref, h_ref, n_rows):
    def blk(r0):
        x = x_ref[pl.ds(r0, 16), :]
        ms = jnp.mean(x * x, axis=-1, keepdims=True)
        h_ref[pl.ds(r0, 16), :] = (x * lax.rsqrt(ms + EPS) * g_ref[...]).astype(h_ref.dtype)
    _rows_loop(n_rows, 16, blk)


def _inproj_kernel(x_ref, g_ref, w_ref, wdt_ref, proj_ref, dt_ref, h_ref):
    @pl.when(pl.program_id(1) == 0)
    def _():
        _rmsnorm_rows(x_ref, g_ref, h_ref, x_ref.shape[0])
        dt_ref[...] = jnp.dot(h_ref[...], wdt_ref[...], preferred_element_type=F32)

    proj_ref[...] = jnp.dot(h_ref[...], w_ref[...], preferred_element_type=F32).astype(proj_ref.dtype)


def _inproj(x, g, w_main, w_dt, tm=1024, tn=1024):
    m, d = x.shape
    n = w_main.shape[1]
    return pl.pallas_call(
        _inproj_kernel,
        grid=(m // tm, n // tn),
        in_specs=[
            pl.BlockSpec((tm, d), lambda i, j: (i, 0)),
            pl.BlockSpec((1, d), lambda i, j: (0, 0)),
            pl.BlockSpec((d, tn), lambda i, j: (0, j)),
            pl.BlockSpec((d, HEAD_PAD), lambda i, j: (0, 0)),
        ],
        out_specs=[
            pl.BlockSpec((tm, tn), lambda i, j: (i, j)),
            pl.BlockSpec((tm, HEAD_PAD), lambda i, j: (i, 0)),
        ],
        out_shape=[
            jax.ShapeDtypeStruct((m, n), BF16),
            jax.ShapeDtypeStruct((m, HEAD_PAD), F32),
        ],
        scratch_shapes=[pltpu.VMEM((tm, d), BF16)],
        compiler_params=_params(("parallel", "arbitrary")),
        name="inproj",
    )(x, g, w_main, w_dt)


HIST_A = 32


def _branch_a_kernel(ua_ref, ub_ref, w_ref, cb_ref, g_ref, b_ref, o_ref, s_ref, sh_ref, cv_ref):
    tt = o_ref.shape[0]
    c = ua_ref.shape[1]
    t = pl.program_id(1)

    @pl.when(t == 0)
    def _():
        s_ref[0:HIST_A, :] = jnp.zeros((HIST_A, c), F32)

    @pl.when(t > 0)
    def _():
        s_ref[0:HIST_A, :] = s_ref[tt:tt + HIST_A, :]

    def glu(r0):
        a = ua_ref[pl.ds(r0, 16), :].astype(F32)
        b = ub_ref[pl.ds(r0, 16), :].astype(F32)
        s_ref[pl.ds(HIST_A + r0, 16), :] = a * jax.nn.sigmoid(b)
    _rows_loop(tt, 16, glu)

    n_sh = sh_ref.shape[1]
    for b in range(1, SUBLANES):
        sh_ref[b - 1] = s_ref[pl.ds(b, n_sh), :]

    lane_blk = 512
    row_blk = 32
    n_lane = c // lane_blk
    base = HIST_A - (K_CONV_A - 1)

    def conv_tile(idx, carry):
        r0 = pl.multiple_of((idx // n_lane) * row_blk, row_blk)
        c0 = pl.multiple_of((idx % n_lane) * lane_blk, lane_blk)
        acc = jnp.broadcast_to(cb_ref[:, pl.ds(c0, lane_blk)], (row_blk, lane_blk))
        for k in range(K_CONV_A):
            a, b = divmod(base + k, SUBLANES)
            rows = pl.ds(r0 + a * SUBLANES, row_blk)
            if b == 0:
                src = s_ref[rows, pl.ds(c0, lane_blk)]
            else:
                src = sh_ref[b - 1, rows, pl.ds(c0, lane_blk)]
            acc = acc + src * w_ref[pl.ds(k, 1), pl.ds(c0, lane_blk)]
        cv_ref[pl.ds(r0, row_blk), pl.ds(c0, lane_blk)] = acc
        return carry
    lax.fori_loop(0, (tt // row_blk) * n_lane, conv_tile, 0)

    def ln_silu(r0):
        v = cv_ref[pl.ds(r0, 16), :]
        mu = jnp.mean(v, axis=-1, keepdims=True)
        vc = v - mu
        var = jnp.mean(vc * vc, axis=-1, keepdims=True)
        y = vc * lax.rsqrt(var + EPS) * g_ref[...] + b_ref[...]
        o_ref[pl.ds(r0, 16), :] = (y * jax.nn.sigmoid(y)).astype(o_ref.dtype)
    _rows_loop(tt, 16, ln_silu)


def _branch_a(proj, conv_w, conv_b, ln_g, ln_b, batch, seq, tt=256):
    m = proj.shape[0]
    c = D_MODEL
    nt = seq // tt
    n_sh = tt + HIST_A - SUBLANES
    row = lambda b, t: (0, 0)
    return pl.pallas_call(
        _branch_a_kernel,
        grid=(batch, nt),
        in_specs=[
            pl.BlockSpec((tt, c), lambda b, t: (b * nt + t, COL_UA // c)),
            pl.BlockSpec((tt, c), lambda b, t: (b * nt + t, COL_UB // c)),
            pl.BlockSpec((HIST_A, c), row),
            pl.BlockSpec((1, c), row),
            pl.BlockSpec((1, c), row),
            pl.BlockSpec((1, c), row),
        ],
        out_specs=pl.BlockSpec((tt, c), lambda b, t: (b * nt + t, 0)),
        out_shape=jax.ShapeDtypeStruct((m, c), BF16),
        scratch_shapes=[
            pltpu.VMEM((tt + HIST_A, c), F32),
            pltpu.VMEM((SUBLANES - 1, n_sh, c), F32),
            pltpu.VMEM((tt, c), F32),
        ],
        compiler_params=_params(("parallel", "arbitrary")),
        name="branch_a",
    )(proj, proj, conv_w, conv_b, ln_g, ln_b)


HIST_B = 8


def _conv_b_kernel(x_ref, w_ref, cb_ref, o_ref, s_ref, sh_ref):
    tt, c = o_ref.shape
    t = pl.program_id(2)

    @pl.when(t == 0)
    def _():
        s_ref[0:HIST_B, :] = jnp.zeros((HIST_B, c), F32)

    @pl.when(t > 0)
    def _():
        s_ref[0:HIST_B, :] = s_ref[tt:tt + HIST_B, :]

    def load(r0):
        s_ref[pl.ds(HIST_B + r0, 16), :] = x_ref[pl.ds(r0, 16), :].astype(F32)
    _rows_loop(tt, 16, load)

    base = HIST_B - (K_CONV_B - 1)
    for k in range(K_CONV_B - 1):
        sh_ref[k] = s_ref[pl.ds(base + k, tt), :]

    def conv(r0):
        acc = cb_ref[...] + s_ref[pl.ds(HIST_B + r0, 16), :] * w_ref[pl.ds(K_CONV_B - 1, 1), :]
        for k in range(K_CONV_B - 1):
            acc = acc + sh_ref[k, pl.ds(r0, 16), :] * w_ref[pl.ds(k, 1), :]
        o_ref[pl.ds(r0, 16), :] = (acc * jax.nn.sigmoid(acc)).astype(o_ref.dtype)
    _rows_loop(tt, 16, conv)


def _conv_b(proj, conv_w, conv_b, batch, seq, tt=512, tc=2048):
    m = proj.shape[0]
    nt = seq // tt
    ncb = CONV_B_DIM // tc
    return pl.pallas_call(
        _conv_b_kernel,
        grid=(batch, ncb, nt),
        in_specs=[
            pl.BlockSpec((tt, tc), lambda b, j, t: (b * nt + t, COL_XBC // tc + j)),
            pl.BlockSpec((SUBLANES, tc), lambda b, j, t: (0, j)),
            pl.BlockSpec((1, tc), lambda b, j, t: (0, j)),
        ],
        out_specs=pl.BlockSpec((tt, tc), lambda b, j, t: (b * nt + t, j)),
        out_shape=jax.ShapeDtypeStruct((m, CONV_B_DIM), BF16),
        scratch_shapes=[
            pltpu.VMEM((tt + HIST_B, tc), F32),
            pltpu.VMEM((K_CONV_B - 1, tt, tc), F32),
        ],
        compiler_params=_params(("parallel", "parallel", "arbitrary")),
        name="conv_b",
    )(proj, conv_w, conv_b)


def _split3(v):
    p1 = v.astype(BF16)
    r1 = v - p1.astype(F32)
    p2 = r1.astype(BF16)
    p3 = (r1 - p2.astype(F32)).astype(BF16)
    return p1, p2, p3


def _dt_kernel(dtraw_ref, bias_ref, alog_ref, acol_ref, rowp_ref, wrow_ref, alast_ref):
    x = dtraw_ref[...] + bias_ref[...]
    dt = jnp.maximum(x, 0.0) + jnp.log1p(jnp.exp(-jnp.abs(x)))
    adt = dt * (-jnp.exp(alog_ref[...]))
    li = lax.broadcasted_iota(jnp.int32, (CHUNK, CHUNK), 0)
    si = lax.broadcasted_iota(jnp.int32, (CHUNK, CHUNK), 1)
    tril = jnp.where(li >= si, 1.0, 0.0).astype(BF16)
    acs = None
    for p in _split3(adt):
        term = jnp.dot(tril, p, preferred_element_type=F32)
        acs = term if acs is None else acs + term
    acol_ref[...] = acs
    alast = acs[CHUNK - 1:CHUNK, :]
    alast_ref[0] = jnp.broadcast_to(alast, (SUBLANES, HEAD_PAD))
    rowp_ref[0] = (acs - jnp.log(dt)).T
    wrow_ref[0] = (jnp.exp(alast - acs) * dt).T


def _dt_tables(dt_raw, dt_bias, a_log):
    m = dt_raw.shape[0]
    nchunks = m // CHUNK
    row = pl.BlockSpec((1, HEAD_PAD), lambda i: (0, 0))
    tab = pl.BlockSpec((1, HEAD_PAD, CHUNK), lambda i: (i, 0, 0))
    return pl.pallas_call(
        _dt_kernel,
        grid=(nchunks,),
        in_specs=[pl.BlockSpec((CHUNK, HEAD_PAD), lambda i: (i, 0)), row, row],
        out_specs=[
            pl.BlockSpec((CHUNK, HEAD_PAD), lambda i: (i, 0)),
            tab,
            tab,
            pl.BlockSpec((1, SUBLANES, HEAD_PAD), lambda i: (i, 0, 0)),
        ],
        out_shape=[
            jax.ShapeDtypeStruct((m, HEAD_PAD), F32),
            jax.ShapeDtypeStruct((nchunks, HEAD_PAD, CHUNK), F32),
            jax.ShapeDtypeStruct((nchunks, HEAD_PAD, CHUNK), F32),
            jax.ShapeDtypeStruct((nchunks, SUBLANES, HEAD_PAD), F32),
        ],
        compiler_params=_params(("parallel",)),
        name="dt_tables",
    )(dt_raw, dt_bias, a_log)


def _ssd_kernel(xbc_ref, acol_ref, rowp_ref, wrow_ref, alast_ref, dskip_ref, y_ref, st_ref):
    @pl.when(pl.program_id(1) == 0)
    def _():
        st_ref[...] = jnp.zeros(st_ref.shape, F32)

    li = lax.broadcasted_iota(jnp.int32, (CHUNK, CHUNK), 0)
    si = lax.broadcasted_iota(jnp.int32, (CHUNK, CHUNK), 1)
    causal = li >= si
    lo = si < HEAD_DIM
    gw = HEADS_PER_GROUP * HEAD_DIM

    for g in range(N_GROUPS):
        bg = xbc_ref[:, D_INNER + g * D_STATE:D_INNER + (g + 1) * D_STATE]
        cg = xbc_ref[:, D_INNER + BC_DIM + g * D_STATE:D_INNER + BC_DIM + (g + 1) * D_STATE]
        cb = lax.dot_general(cg, bg, (((1,), (1,)), ((), ())), preferred_element_type=F32)
        bgt = bg.astype(F32).T
        s_old = st_ref[:, g * gw:(g + 1) * gw]
        y_off = jnp.dot(cg, s_old.astype(BF16), preferred_element_type=F32)
        decay = jnp.exp(alast_ref[0, 0:1, g * gw:(g + 1) * gw])
        for j in range(HEADS_PER_GROUP // 2):
            c0 = g * gw + j * LANES
            xp = xbc_ref[:, c0:c0 + LANES].astype(F32)
            xbd = jnp.concatenate([jnp.where(lo, xp, 0.0), jnp.where(lo, 0.0, xp)],
                                  axis=0).astype(BF16)
            ms, ws, cols = [], [], []
            for h in (g * HEADS_PER_GROUP + 2 * j, g * HEADS_PER_GROUP + 2 * j + 1):
                colb = jnp.broadcast_to(acol_ref[:, h:h + 1], (CHUNK, CHUNK))
                seg = colb - rowp_ref[0, h:h + 1, :]
                lmat = jnp.exp(jnp.where(causal, seg, -jnp.inf))
                ms.append((cb * lmat).astype(BF16))
                ws.append((bgt * wrow_ref[0, h:h + 1, :]).astype(BF16))
                cols.append(colb)
            y_diag = jnp.dot(jnp.concatenate(ms, axis=1), xbd, preferred_element_type=F32)
            s_new = jnp.dot(jnp.concatenate(ws, axis=1), xbd, preferred_element_type=F32)
            e = jnp.exp(jnp.where(lo, cols[0], cols[1]))
            jl = slice(j * LANES, (j + 1) * LANES)
            y = y_diag + y_off[:, jl] * e + dskip_ref[:, c0:c0 + LANES] * xp
            y_ref[:, c0:c0 + LANES] = y.astype(y_ref.dtype)
            st_ref[:, c0:c0 + LANES] = s_old[:, jl] * decay[:, jl] + s_new


def _ssd(xbc, acol, rowp, wrow, alast_exp, dskip_exp, batch, seq):
    m = xbc.shape[0]
    nc = seq // CHUNK
    tab = pl.BlockSpec((1, HEAD_PAD, CHUNK), lambda b, c: (b * nc + c, 0, 0))
    return pl.pallas_call(
        _ssd_kernel,
        grid=(batch, nc),
        in_specs=[
            pl.BlockSpec((CHUNK, CONV_B_DIM), lambda b, c: (b * nc + c, 0)),
            pl.BlockSpec((CHUNK, HEAD_PAD), lambda b, c: (b * nc + c, 0)),
            tab,
            tab,
            pl.BlockSpec((1, SUBLANES, D_INNER), lambda b, c: (b * nc + c, 0, 0)),
            pl.BlockSpec((1, D_INNER), lambda b, c: (0, 0)),
        ],
        out_specs=pl.BlockSpec((CHUNK, D_INNER), lambda b, c: (b * nc + c, 0)),
        out_shape=jax.ShapeDtypeStruct((m, D_INNER), BF16),
        scratch_shapes=[pltpu.VMEM((D_STATE, D_INNER), F32)],
        compiler_params=_params(("parallel", "arbitrary")),
        name="ssd",
    )(xbc, acol, rowp, wrow, alast_exp, dskip_exp)


def _merge_kernel(y_ref, z_ref, ng_ref, u_ref, ga_ref, gb_ref, wss_ref, wco_ref, o_ref, v_ref):
    @pl.when(pl.program_id(1) == 0)
    def _():
        gsz = D_INNER // N_GROUPS

        def blk(r0):
            for g in range(N_GROUPS):
                cs = slice(g * gsz, (g + 1) * gsz)
                z = z_ref[pl.ds(r0, 16), cs].astype(F32)
                v = y_ref[pl.ds(r0, 16), cs].astype(F32) * (z * jax.nn.sigmoid(z))
                ms = jnp.mean(v * v, axis=-1, keepdims=True)
                v_ref[pl.ds(r0, 16), cs] = (v * lax.rsqrt(ms + EPS) * ng_ref[:, cs]).astype(BF16)
        _rows_loop(y_ref.shape[0], 16, blk)

    y_b = jnp.dot(v_ref[...], wss_ref[...], preferred_element_type=F32)
    y_a = jnp.dot(u_ref[...], wco_ref[...], preferred_element_type=F32)
    g_a = jax.nn.sigmoid(ga_ref[...].astype(F32))
    g_b = jax.nn.sigmoid(gb_ref[...].astype(F32))
    o_ref[...] = (g_a * y_a + g_b * y_b).astype(o_ref.dtype)


def _merge(y, proj, norm_g, u, w_ssm_out, w_conv_out, tm=512, tn=512):
    m = y.shape[0]
    n = D_MODEL
    return pl.pallas_call(
        _merge_kernel,
        grid=(m // tm, n // tn),
        in_specs=[
            pl.BlockSpec((tm, D_INNER), lambda i, j: (i, 0)),
            pl.BlockSpec((tm, D_INNER), lambda i, j: (i, COL_Z // D_INNER)),
            pl.BlockSpec((1, D_INNER), lambda i, j: (0, 0)),
            pl.BlockSpec((tm, D_MODEL), lambda i, j: (i, 0)),
            pl.BlockSpec((tm, tn), lambda i, j: (i, COL_GA // tn + j)),
            pl.BlockSpec((tm, tn), lambda i, j: (i, COL_GB // tn + j)),
            pl.BlockSpec((D_INNER, tn), lambda i, j: (0, j)),
            pl.BlockSpec((D_MODEL, tn), lambda i, j: (0, j)),
        ],
        out_specs=pl.BlockSpec((tm, tn), lambda i, j: (i, j)),
        out_shape=jax.ShapeDtypeStruct((m, n), BF16),
        scratch_shapes=[pltpu.VMEM((tm, D_INNER), BF16)],
        compiler_params=_params(("parallel", "arbitrary")),
        name="merge",
    )(y, proj, norm_g, u, proj, proj, w_ssm_out, w_conv_out)


def _wo_kernel(a_ref, w_ref, x_ref, o_ref):
    o_ref[...] = x_ref[...] + jnp.dot(a_ref[...], w_ref[...], preferred_element_type=F32)


def _wo(a, w, x, tm=1024, tn=1024):
    m, k = a.shape
    n = w.shape[1]
    return pl.pallas_call(
        _wo_kernel,
        grid=(m // tm, n // tn),
        in_specs=[
            pl.BlockSpec((tm, k), lambda i, j: (i, 0)),
            pl.BlockSpec((k, tn), lambda i, j: (0, j)),
            pl.BlockSpec((tm, tn), lambda i, j: (i, j)),
        ],
        out_specs=pl.BlockSpec((tm, tn), lambda i, j: (i, j)),
        out_shape=jax.ShapeDtypeStruct((m, n), F32),
        compiler_params=_params(("parallel", "parallel")),
        name="wo",
    )(a, w, x)


def _ffn_up_kernel(x_ref, g_ref, wg_ref, wu_ref, o_ref, h_ref):
    @pl.when(pl.program_id(1) == 0)
    def _():
        _rmsnorm_rows(x_ref, g_ref, h_ref, x_ref.shape[0])

    h = h_ref[...]
    a = jnp.dot(h, wg_ref[...], preferred_element_type=F32)
    b = jnp.dot(h, wu_ref[...], preferred_element_type=F32)
    o_ref[...] = (a * jax.nn.sigmoid(a) * b).astype(o_ref.dtype)


def _ffn_up(x, g, w_gate, w_up, tm=1024, tn=512):
    m, d = x.shape
    n = w_gate.shape[1]
    return pl.pallas_call(
        _ffn_up_kernel,
        grid=(m // tm, n // tn),
        in_specs=[
            pl.BlockSpec((tm, d), lambda i, j: (i, 0)),
            pl.BlockSpec((1, d), lambda i, j: (0, 0)),
            pl.BlockSpec((d, tn), lambda i, j: (0, j)),
            pl.BlockSpec((d, tn), lambda i, j: (0, j)),
        ],
        out_specs=pl.BlockSpec((tm, tn), lambda i, j: (i, j)),
        out_shape=jax.ShapeDtypeStruct((m, n), BF16),
        scratch_shapes=[pltpu.VMEM((tm, d), BF16)],
        compiler_params=_params(("parallel", "arbitrary")),
        name="ffn_up",
    )(x, g, w_gate, w_up)


def _ffn_down_kernel(a_ref, w_ref, x_ref, g_ref, o_ref, acc_ref, *, final_norm):
    k = pl.program_id(1)

    @pl.when(k == 0)
    def _():
        acc_ref[...] = x_ref[...]

    acc_ref[...] += jnp.dot(a_ref[...], w_ref[...], preferred_element_type=F32)

    @pl.when(k == pl.num_programs(1) - 1)
    def _():
        if final_norm:
            _rmsnorm_rows(acc_ref, g_ref, o_ref, acc_ref.shape[0])
        else:
            o_ref[...] = acc_ref[...]


def _ffn_down(a, w, x, g, final_norm, tm=512, tk=1408):
    m, kdim = a.shape
    n = w.shape[1]
    return pl.pallas_call(
        functools.partial(_ffn_down_kernel, final_norm=final_norm),
        grid=(m // tm, kdim // tk),
        in_specs=[
            pl.BlockSpec((tm, tk), lambda i, k: (i, k)),
            pl.BlockSpec((tk, n), lambda i, k: (k, 0)),
            pl.BlockSpec((tm, n), lambda i, k: (i, 0)),
            pl.BlockSpec((1, n), lambda i, k: (0, 0)),
        ],
        out_specs=pl.BlockSpec((tm, n), lambda i, k: (i, 0)),
        out_shape=jax.ShapeDtypeStruct((m, n), F32),
        scratch_shapes=[pltpu.VMEM((tm, n), F32)],
        compiler_params=_params(("parallel", "arbitrary")),
        name="ffn_down",
    )(a, w, x, g)


def _pad_heads(v):
    return jnp.pad(v.astype(F32), (0, HEAD_PAD - N_HEADS)).reshape(1, HEAD_PAD)


def _pack_w_in(w):
    dt0 = 2 * D_MODEL + D_INNER + CONV_B_DIM
    w_main = jnp.concatenate([w[:, :dt0], w[:, dt0 + N_HEADS:]], axis=1).astype(BF16)
    w_dt = jnp.pad(w[:, dt0:dt0 + N_HEADS], ((0, 0), (0, HEAD_PAD - N_HEADS))).astype(BF16)
    return w_main, w_dt


def _row(v):
    return v.astype(F32).reshape(1, -1)


def kernel(x, norm_mix_g, w_in, conv_a_w, conv_a_b, ln_a_g, ln_a_b, w_conv_out, conv_b_w, conv_b_b, dt_bias, a_log, d_skip, ssm_norm_g, w_ssm_out, w_o, norm_ffn_g, w_gate, w_up, w_down, final_g):
    batch, seq, d = x.shape
    depth = w_in.shape[0]
    xf = x.reshape(batch * seq, d)
    nchunks = batch * seq // CHUNK
    for l in range(depth):
        w_main, w_dt = _pack_w_in(w_in[l])
        proj, dt_raw = _inproj(xf, _row(norm_mix_g[l]), w_main, w_dt)
        cw = jnp.pad(conv_a_w[l].astype(F32), ((0, HIST_A - K_CONV_A), (0, 0)))
        u = _branch_a(proj, cw, _row(conv_a_b[l]), _row(ln_a_g[l]), _row(ln_a_b[l]), batch, seq)
        cwb = jnp.pad(conv_b_w[l].astype(F32), ((0, SUBLANES - K_CONV_B), (0, 0)))
        xbc = _conv_b(proj, cwb, _row(conv_b_b[l]), batch, seq)
        acol, rowp, wrow, alast = _dt_tables(dt_raw, _pad_heads(dt_bias[l]), _pad_heads(a_log[l]))
        alast_exp = jnp.repeat(alast[:, :, :N_HEADS], HEAD_DIM, axis=2)
        dskip_exp = jnp.repeat(d_skip[l].astype(F32), HEAD_DIM).reshape(1, D_INNER)
        y = _ssd(xbc, acol, rowp, wrow, alast_exp, dskip_exp, batch, seq)
        merged = _merge(y, proj, _row(ssm_norm_g[l]), u, w_ssm_out[l].astype(BF16),
                        w_conv_out[l].astype(BF16))
        xf = _wo(merged, w_o[l].astype(BF16), xf)
        act = _ffn_up(xf, _row(norm_ffn_g[l]), w_gate[l].astype(BF16), w_up[l].astype(BF16))
        last = l == depth - 1
        xf = _ffn_down(act, w_down[l].astype(BF16), xf, _row(final_g), final_norm=last)
    return xf.reshape(batch, seq, d)
```

```python
import functools

import jax
import jax.numpy as jnp
from jax import lax
from jax.experimental import pallas as pl
from jax.experimental.pallas import tpu as pltpu

F32 = jnp.float32
BF16 = jnp.bfloat16

D_MODEL = 2048
K_CONV_A = 31
D_INNER = 2 * D_MODEL
HEAD_DIM = 64
N_HEADS = D_INNER // HEAD_DIM
N_GROUPS = 8
HEADS_PER_GROUP = N_HEADS // N_GROUPS
D_STATE = 128
K_CONV_B = 4
BC_DIM = N_GROUPS * D_STATE
CONV_B_DIM = D_INNER + 2 * BC_DIM
CHUNK = 128
EPS = 1e-6

COL_GLU = 0
COL_Z = 2 * D_MODEL
COL_XBC = COL_Z + D_INNER
COL_DT = COL_XBC + CONV_B_DIM
COL_GATES = COL_DT + N_HEADS

LANES = 128
SUBLANES = 8
HEAD_PAD = LANES
VMEM_LIMIT = 56 * 1024 * 1024


def _params(sem):
    return pltpu.CompilerParams(dimension_semantics=sem, vmem_limit_bytes=VMEM_LIMIT)


def _rows_loop(n_rows, blk, fn, unroll=1):
    def body(i, carry):
        fn(pl.multiple_of(i * blk, blk))
        return carry
    lax.fori_loop(0, n_rows // blk, body, 0, unroll=unroll)


def _cast_rows(src_ref, dst_ref, col0=0):
    w = src_ref.shape[1]
    def blk(r0):
        dst_ref[pl.ds(r0, 32), col0:col0 + w] = src_ref[pl.ds(r0, 32), :].astype(BF16)
    _rows_loop(src_ref.shape[0], 32, blk)


def _rmsnorm_rows(x_ref, g_ref, h_ref, n_rows):
    def blk(r0):
        x = x_ref[pl.ds(r0, 16), :]
        ms = jnp.mean(x * x, axis=-1, keepdims=True)
        h_ref[pl.ds(r0, 16), :] = (x * lax.rsqrt(ms + EPS) * g_ref[...]).astype(h_ref.dtype)
    _rows_loop(n_rows, 16, blk, unroll=2)


def _silu(v):
    return v * jax.nn.sigmoid(v)


def _rmsnorm_kernel(x_ref, g_ref, h_ref):
    _rmsnorm_rows(x_ref, g_ref, h_ref, x_ref.shape[0])


def _rmsnorm(x, g, tm=512):
    m, d = x.shape
    return pl.pallas_call(
        _rmsnorm_kernel,
        grid=(m // tm,),
        in_specs=[pl.BlockSpec((tm, d), lambda i: (i, 0)), pl.BlockSpec((1, d), lambda i: (0, 0))],
        out_specs=pl.BlockSpec((tm, d), lambda i: (i, 0)),
        out_shape=jax.ShapeDtypeStruct((m, d), BF16),
        compiler_params=_params(("parallel",)),
        name="rmsnorm",
    )(x, g)


def _proj_glu_kernel(h_ref, wa_ref, wb_ref, o_ref, wbf_ref):
    @pl.when(pl.program_id(1) == 0)
    def _():
        _cast_rows(wa_ref, wbf_ref, 0)
        _cast_rows(wb_ref, wbf_ref, wa_ref.shape[1])

    tn = o_ref.shape[1]
    r = jnp.dot(h_ref[...], wbf_ref[...], preferred_element_type=F32)
    o_ref[...] = (r[:, :tn] * jax.nn.sigmoid(r[:, tn:])).astype(o_ref.dtype)


def _proj_glu(h, w_in, layer, tm=1024, tn=512):
    m, d = h.shape
    nb = D_MODEL // tn
    return pl.pallas_call(
        _proj_glu_kernel,
        grid=(nb, m // tm),
        in_specs=[
            pl.BlockSpec((tm, d), lambda j, i: (i, 0)),
            pl.BlockSpec((None, d, tn), lambda j, i: (layer, 0, COL_GLU // tn + j)),
            pl.BlockSpec((None, d, tn), lambda j, i: (layer, 0, COL_GLU // tn + nb + j)),
        ],
        out_specs=pl.BlockSpec((tm, tn), lambda j, i: (i, j)),
        out_shape=jax.ShapeDtypeStruct((m, D_MODEL), BF16),
        scratch_shapes=[pltpu.VMEM((d, 2 * tn), BF16)],
        compiler_params=_params(("parallel", "arbitrary")),
        name="proj_glu",
    )(h, w_in, w_in)


def _proj_act_kernel(h_ref, w_ref, o_ref, wbf_ref, *, act):
    @pl.when(pl.program_id(1) == 0)
    def _():
        _cast_rows(w_ref, wbf_ref)

    r = jnp.dot(h_ref[...], wbf_ref[...], preferred_element_type=F32)
    o_ref[...] = act(r).astype(o_ref.dtype)


def _proj_act(h, w, w_block, n_out, act, name, tm=1024, tn=1024):
    m, d = h.shape
    w_shape = (None,) * (w.ndim - 2) + (d, tn)
    return pl.pallas_call(
        functools.partial(_proj_act_kernel, act=act),
        grid=(n_out // tn, m // tm),
        in_specs=[
            pl.BlockSpec((tm, d), lambda j, i: (i, 0)),
            pl.BlockSpec(w_shape, lambda j, i: w_block(j)),
        ],
        out_specs=pl.BlockSpec((tm, tn), lambda j, i: (i, j)),
        out_shape=jax.ShapeDtypeStruct((m, n_out), BF16),
        scratch_shapes=[pltpu.VMEM((d, tn), BF16)],
        compiler_params=_params(("parallel", "arbitrary")),
        name=name,
    )(h, w)


def _proj_xbc_kernel(h_ref, w_ref, cw_ref, cb_ref, o_ref, wbf_ref, hist_ref, *, tiles_per_seq):
    i = pl.program_id(1)

    @pl.when(i == 0)
    def _():
        _cast_rows(w_ref, wbf_ref)

    @pl.when(i % tiles_per_seq == 0)
    def _():
        hist_ref[...] = jnp.zeros(hist_ref.shape, F32)

    tm = o_ref.shape[0]
    r = jnp.dot(h_ref[...], wbf_ref[...], preferred_element_type=F32)
    ext = jnp.concatenate([hist_ref[...], r], axis=0)
    acc = cb_ref[...] + r * cw_ref[K_CONV_B - 1:K_CONV_B, :]
    for k in range(K_CONV_B - 1):
        off = SUBLANES - (K_CONV_B - 1) + k
        acc = acc + ext[off:off + tm, :] * cw_ref[k:k + 1, :]
    o_ref[...] = _silu(acc).astype(o_ref.dtype)
    hist_ref[...] = r[tm - SUBLANES:, :]


def _proj_xbc(h, w_in, layer, conv_w, conv_b, seq, tm=1024, tn=1024):
    m, d = h.shape
    return pl.pallas_call(
        functools.partial(_proj_xbc_kernel, tiles_per_seq=seq // tm),
        grid=(CONV_B_DIM // tn, m // tm),
        in_specs=[
            pl.BlockSpec((tm, d), lambda j, i: (i, 0)),
            pl.BlockSpec((None, d, tn), lambda j, i: (layer, 0, COL_XBC // tn + j)),
            pl.BlockSpec((SUBLANES, tn), lambda j, i: (0, j)),
            pl.BlockSpec((1, tn), lambda j, i: (0, j)),
        ],
        out_specs=pl.BlockSpec((tm, tn), lambda j, i: (i, j)),
        out_shape=jax.ShapeDtypeStruct((m, CONV_B_DIM), BF16),
        scratch_shapes=[pltpu.VMEM((d, tn), BF16), pltpu.VMEM((SUBLANES, tn), F32)],
        compiler_params=_params(("parallel", "arbitrary")),
        name="proj_xbc",
    )(h, w_in, conv_w, conv_b)


HIST_A = 32


def _branch_a_kernel(u_ref, w_ref, cb_ref, g_ref, b_ref, o_ref, s_ref, sh_ref, cv_ref):
    tt, c = o_ref.shape
    t = pl.program_id(1)

    @pl.when(t == 0)
    def _():
        s_ref[0:HIST_A, :] = jnp.zeros((HIST_A, c), F32)

    @pl.when(t > 0)
    def _():
        s_ref[0:HIST_A, :] = s_ref[tt:tt + HIST_A, :]

    def load(r0):
        s_ref[pl.ds(HIST_A + r0, 16), :] = u_ref[pl.ds(r0, 16), :].astype(F32)
    _rows_loop(tt, 16, load)

    n_sh = sh_ref.shape[1]
    for b in range(1, SUBLANES):
        sh_ref[b - 1] = s_ref[pl.ds(b, n_sh), :]

    lane_blk = 256
    row_blk = 32
    n_lane = c // lane_blk
    base = HIST_A - (K_CONV_A - 1)

    def conv_tile(idx, carry):
        r0 = pl.multiple_of((idx // n_lane) * row_blk, row_blk)
        c0 = pl.multiple_of((idx % n_lane) * lane_blk, lane_blk)
        acc = jnp.broadcast_to(cb_ref[:, pl.ds(c0, lane_blk)], (row_blk, lane_blk))
        for k in range(K_CONV_A):
            a, b = divmod(base + k, SUBLANES)
            rows = pl.ds(r0 + a * SUBLANES, row_blk)
            if b == 0:
                src = s_ref[rows, pl.ds(c0, lane_blk)]
            else:
                src = sh_ref[b - 1, rows, pl.ds(c0, lane_blk)]
            acc = acc + src * w_ref[pl.ds(k, 1), pl.ds(c0, lane_blk)]
        cv_ref[pl.ds(r0, row_blk), pl.ds(c0, lane_blk)] = acc
        return carry
    lax.fori_loop(0, (tt // row_blk) * n_lane, conv_tile, 0)

    def ln_silu(r0):
        v = cv_ref[pl.ds(r0, 8), :]
        mu = jnp.mean(v, axis=-1, keepdims=True)
        vc = v - mu
        var = jnp.mean(vc * vc, axis=-1, keepdims=True)
        y = vc * lax.rsqrt(var + EPS) * g_ref[...] + b_ref[...]
        o_ref[pl.ds(r0, 8), :] = _silu(y).astype(o_ref.dtype)
    _rows_loop(tt, 8, ln_silu, unroll=4)


def _branch_a(u, conv_w, conv_b, ln_g, ln_b, batch, seq, tt=256):
    m, c = u.shape
    nt = seq // tt
    n_sh = tt + HIST_A - SUBLANES
    row = lambda b, t: (0, 0)
    return pl.pallas_call(
        _branch_a_kernel,
        grid=(batch, nt),
        in_specs=[
            pl.BlockSpec((tt, c), lambda b, t: (b * nt + t, 0)),
            pl.BlockSpec((HIST_A, c), row),
            pl.BlockSpec((1, c), row),
            pl.BlockSpec((1, c), row),
            pl.BlockSpec((1, c), row),
        ],
        out_specs=pl.BlockSpec((tt, c), lambda b, t: (b * nt + t, 0)),
        out_shape=jax.ShapeDtypeStruct((m, c), BF16),
        scratch_shapes=[
            pltpu.VMEM((tt + HIST_A, c), F32),
            pltpu.VMEM((SUBLANES - 1, n_sh, c), F32),
            pltpu.VMEM((tt, c), F32),
        ],
        compiler_params=_params(("parallel", "arbitrary")),
        name="branch_a",
    )(u, conv_w, conv_b, ln_g, ln_b)


def _split3(v):
    p1 = v.astype(BF16)
    r1 = v - p1.astype(F32)
    p2 = r1.astype(BF16)
    p3 = (r1 - p2.astype(F32)).astype(BF16)
    return p1, p2, p3


def _dt_kernel(h_ref, wdt_ref, bias_ref, alog_ref, acol_ref, rowp_ref, wrow_ref, alast_ref):
    x = jnp.dot(h_ref[...], wdt_ref[...], preferred_element_type=F32) + bias_ref[...]
    dt = jnp.maximum(x, 0.0) + jnp.log1p(jnp.exp(-jnp.abs(x)))
    adt = dt * (-jnp.exp(alog_ref[...]))
    li = lax.broadcasted_iota(jnp.int32, (CHUNK, CHUNK), 0)
    si = lax.broadcasted_iota(jnp.int32, (CHUNK, CHUNK), 1)
    tril = jnp.where(li >= si, 1.0, 0.0).astype(BF16)
    acs = None
    for p in _split3(adt):
        term = jnp.dot(tril, p, preferred_element_type=F32)
        acs = term if acs is None else acs + term
    acol_ref[...] = acs
    alast = acs[CHUNK - 1:CHUNK, :]
    alast_ref[0] = jnp.broadcast_to(alast, (SUBLANES, HEAD_PAD))
    rowp_ref[0] = (acs - jnp.log(dt)).T
    wrow_ref[0] = (jnp.exp(alast - acs) * dt).T


def _dt_tables(h, w_dt, dt_bias, a_log):
    m, d = h.shape
    nchunks = m // CHUNK
    row = pl.BlockSpec((1, HEAD_PAD), lambda i: (0, 0))
    tab = pl.BlockSpec((1, HEAD_PAD, CHUNK), lambda i: (i, 0, 0))
    return pl.pallas_call(
        _dt_kernel,
        grid=(nchunks,),
        in_specs=[
            pl.BlockSpec((CHUNK, d), lambda i: (i, 0)),
            pl.BlockSpec((d, HEAD_PAD), lambda i: (0, 0)),
            row,
            row,
        ],
        out_specs=[
            pl.BlockSpec((CHUNK, HEAD_PAD), lambda i: (i, 0)),
            tab,
            tab,
            pl.BlockSpec((1, SUBLANES, HEAD_PAD), lambda i: (i, 0, 0)),
        ],
        out_shape=[
            jax.ShapeDtypeStruct((m, HEAD_PAD), F32),
            jax.ShapeDtypeStruct((nchunks, HEAD_PAD, CHUNK), F32),
            jax.ShapeDtypeStruct((nchunks, HEAD_PAD, CHUNK), F32),
            jax.ShapeDtypeStruct((nchunks, SUBLANES, HEAD_PAD), F32),
        ],
        compiler_params=_params(("parallel",)),
        name="dt_tables",
    )(h, w_dt, dt_bias, a_log)


def _ssd_kernel(xbc_ref, acol_ref, rowp_ref, wrow_ref, alast_ref, dskip_ref, y_ref, st_ref):
    @pl.when(pl.program_id(1) == 0)
    def _():
        st_ref[...] = jnp.zeros(st_ref.shape, F32)

    li = lax.broadcasted_iota(jnp.int32, (CHUNK, CHUNK), 0)
    si = lax.broadcasted_iota(jnp.int32, (CHUNK, CHUNK), 1)
    causal = li >= si
    lo = si < HEAD_DIM
    gw = HEADS_PER_GROUP * HEAD_DIM

    for g in range(N_GROUPS):
        bg = xbc_ref[:, D_INNER + g * D_STATE:D_INNER + (g + 1) * D_STATE]
        cg = xbc_ref[:, D_INNER + BC_DIM + g * D_STATE:D_INNER + BC_DIM + (g + 1) * D_STATE]
        cb = lax.dot_general(cg, bg, (((1,), (1,)), ((), ())), preferred_element_type=F32)
        bgt = bg.astype(F32).T
        s_old = st_ref[:, g * gw:(g + 1) * gw]
        y_off = jnp.dot(cg, s_old.astype(BF16), preferred_element_type=F32)
        decay = jnp.exp(alast_ref[0, 0:1, g * gw:(g + 1) * gw])
        for j in range(HEADS_PER_GROUP // 2):
            c0 = g * gw + j * LANES
            xp = xbc_ref[:, c0:c0 + LANES].astype(F32)
            xbd = jnp.concatenate([jnp.where(lo, xp, 0.0), jnp.where(lo, 0.0, xp)],
                                  axis=0).astype(BF16)
            ms, ws, cols = [], [], []
            for h in (g * HEADS_PER_GROUP + 2 * j, g * HEADS_PER_GROUP + 2 * j + 1):
                colb = jnp.broadcast_to(acol_ref[:, h:h + 1], (CHUNK, CHUNK))
                seg = colb - rowp_ref[0, h:h + 1, :]
                lmat = jnp.exp(jnp.where(causal, seg, -jnp.inf))
                ms.append((cb * lmat).astype(BF16))
                ws.append((bgt * wrow_ref[0, h:h + 1, :]).astype(BF16))
                cols.append(colb)
            y_diag = jnp.dot(jnp.concatenate(ms, axis=1), xbd, preferred_element_type=F32)
            s_new = jnp.dot(jnp.concatenate(ws, axis=1), xbd, preferred_element_type=F32)
            e = jnp.exp(jnp.where(lo, cols[0], cols[1]))
            jl = slice(j * LANES, (j + 1) * LANES)
            y = y_diag + y_off[:, jl] * e + dskip_ref[:, c0:c0 + LANES] * xp
            y_ref[:, c0:c0 + LANES] = y.astype(y_ref.dtype)
            st_ref[:, c0:c0 + LANES] = s_old[:, jl] * decay[:, jl] + s_new


def _ssd(xbc, acol, rowp, wrow, alast_exp, dskip_exp, batch, seq):
    m = xbc.shape[0]
    nc = seq // CHUNK
    tab = pl.BlockSpec((1, HEAD_PAD, CHUNK), lambda b, c: (b * nc + c, 0, 0))
    return pl.pallas_call(
        _ssd_kernel,
        grid=(batch, nc),
        in_specs=[
            pl.BlockSpec((CHUNK, CONV_B_DIM), lambda b, c: (b * nc + c, 0)),
            pl.BlockSpec((CHUNK, HEAD_PAD), lambda b, c: (b * nc + c, 0)),
            tab,
            tab,
            pl.BlockSpec((1, SUBLANES, D_INNER), lambda b, c: (b * nc + c, 0, 0)),
            pl.BlockSpec((1, D_INNER), lambda b, c: (0, 0)),
        ],
        out_specs=pl.BlockSpec((CHUNK, D_INNER), lambda b, c: (b * nc + c, 0)),
        out_shape=jax.ShapeDtypeStruct((m, D_INNER), BF16),
        scratch_shapes=[pltpu.VMEM((D_STATE, D_INNER), F32)],
        compiler_params=_params(("parallel", "arbitrary")),
        name="ssd",
    )(xbc, acol, rowp, wrow, alast_exp, dskip_exp)


def _merge_kernel(y_ref, zs_ref, ng_ref, u_ref, ga_ref, gb_ref, wss_ref, wco_ref, o_ref, v_ref):
    @pl.when(pl.program_id(1) == 0)
    def _():
        gsz = D_INNER // N_GROUPS

        def blk(r0):
            for g in range(N_GROUPS):
                cs = slice(g * gsz, (g + 1) * gsz)
                v = y_ref[pl.ds(r0, 16), cs].astype(F32) * zs_ref[pl.ds(r0, 16), cs].astype(F32)
                ms = jnp.mean(v * v, axis=-1, keepdims=True)
                v_ref[pl.ds(r0, 16), cs] = (v * lax.rsqrt(ms + EPS) * ng_ref[:, cs]).astype(BF16)
        _rows_loop(y_ref.shape[0], 16, blk)

    y_b = jnp.dot(v_ref[...], wss_ref[...], preferred_element_type=F32)
    y_a = jnp.dot(u_ref[...], wco_ref[...], preferred_element_type=F32)
    o_ref[...] = (ga_ref[...].astype(F32) * y_a + gb_ref[...].astype(F32) * y_b).astype(o_ref.dtype)


def _merge(y, zs, norm_g, u, gates, w_ssm_out, w_conv_out, tm=512, tn=512):
    m = y.shape[0]
    n = D_MODEL
    return pl.pallas_call(
        _merge_kernel,
        grid=(m // tm, n // tn),
        in_specs=[
            pl.BlockSpec((tm, D_INNER), lambda i, j: (i, 0)),
            pl.BlockSpec((tm, D_INNER), lambda i, j: (i, 0)),
            pl.BlockSpec((1, D_INNER), lambda i, j: (0, 0)),
            pl.BlockSpec((tm, D_MODEL), lambda i, j: (i, 0)),
            pl.BlockSpec((tm, tn), lambda i, j: (i, j)),
            pl.BlockSpec((tm, tn), lambda i, j: (i, n // tn + j)),
            pl.BlockSpec((D_INNER, tn), lambda i, j: (0, j)),
            pl.BlockSpec((D_MODEL, tn), lambda i, j: (0, j)),
        ],
        out_specs=pl.BlockSpec((tm, tn), lambda i, j: (i, j)),
        out_shape=jax.ShapeDtypeStruct((m, n), BF16),
        scratch_shapes=[pltpu.VMEM((tm, D_INNER), BF16)],
        compiler_params=_params(("parallel", "arbitrary")),
        name="merge",
    )(y, zs, norm_g, u, gates, gates, w_ssm_out, w_conv_out)


def _wo_kernel(a_ref, w_ref, x_ref, g_ref, xo_ref, h_ref, wbf_ref):
    @pl.when(pl.program_id(0) == 0)
    def _():
        _cast_rows(w_ref, wbf_ref)

    xo_ref[...] = x_ref[...] + jnp.dot(a_ref[...], wbf_ref[...], preferred_element_type=F32)
    _rmsnorm_rows(xo_ref, g_ref, h_ref, xo_ref.shape[0])


def _wo(a, w_o, layer, x, g, tm=256):
    m, k = a.shape
    n = D_MODEL
    return pl.pallas_call(
        _wo_kernel,
        grid=(m // tm,),
        in_specs=[
            pl.BlockSpec((tm, k), lambda i: (i, 0)),
            pl.BlockSpec((None, k, n), lambda i: (layer, 0, 0), pipeline_mode=pl.Buffered(1)),
            pl.BlockSpec((tm, n), lambda i: (i, 0)),
            pl.BlockSpec((1, n), lambda i: (0, 0)),
        ],
        out_specs=[
            pl.BlockSpec((tm, n), lambda i: (i, 0)),
            pl.BlockSpec((tm, n), lambda i: (i, 0)),
        ],
        out_shape=[jax.ShapeDtypeStruct((m, n), F32), jax.ShapeDtypeStruct((m, n), BF16)],
        scratch_shapes=[pltpu.VMEM((k, n), BF16)],
        compiler_params=_params(("arbitrary",)),
        name="wo",
    )(a, w_o, x, g)


def _ffn_up_kernel(h_ref, wg_ref, wu_ref, o_ref, wbf_ref):
    @pl.when(pl.program_id(1) == 0)
    def _():
        _cast_rows(wg_ref, wbf_ref, 0)
        _cast_rows(wu_ref, wbf_ref, wg_ref.shape[1])

    tn = o_ref.shape[1]
    r = jnp.dot(h_ref[...], wbf_ref[...], preferred_element_type=F32)
    o_ref[...] = (_silu(r[:, :tn]) * r[:, tn:]).astype(o_ref.dtype)


def _ffn_up(h, w_gate, w_up, layer, tm=1024, tn=512):
    m, d = h.shape
    n = w_gate.shape[2]
    wspec = pl.BlockSpec((None, d, tn), lambda j, i: (layer, 0, j))
    return pl.pallas_call(
        _ffn_up_kernel,
        grid=(n // tn, m // tm),
        in_specs=[pl.BlockSpec((tm, d), lambda j, i: (i, 0)), wspec, wspec],
        out_specs=pl.BlockSpec((tm, tn), lambda j, i: (i, j)),
        out_shape=jax.ShapeDtypeStruct((m, n), BF16),
        scratch_shapes=[pltpu.VMEM((d, 2 * tn), BF16)],
        compiler_params=_params(("parallel", "arbitrary")),
        name="ffn_up",
    )(h, w_gate, w_up)


def _ffn_down_kernel(a_ref, w_ref, x_ref, g_ref, xo_ref, h_ref, acc_ref):
    k = pl.program_id(1)

    @pl.when(k == 0)
    def _():
        acc_ref[...] = x_ref[...]

    acc_ref[...] += jnp.dot(a_ref[...], w_ref[...], preferred_element_type=F32)

    @pl.when(k == pl.num_programs(1) - 1)
    def _():
        xo_ref[...] = acc_ref[...]
        _rmsnorm_rows(acc_ref, g_ref, h_ref, acc_ref.shape[0])


def _ffn_down(a, w, x, g, h_dtype, tm=512, tk=1408):
    m, kdim = a.shape
    n = w.shape[1]
    return pl.pallas_call(
        _ffn_down_kernel,
        grid=(m // tm, kdim // tk),
        in_specs=[
            pl.BlockSpec((tm, tk), lambda i, k: (i, k)),
            pl.BlockSpec((tk, n), lambda i, k: (k, 0)),
            pl.BlockSpec((tm, n), lambda i, k: (i, 0)),
            pl.BlockSpec((1, n), lambda i, k: (0, 0)),
        ],
        out_specs=[
            pl.BlockSpec((tm, n), lambda i, k: (i, 0)),
            pl.BlockSpec((tm, n), lambda i, k: (i, 0)),
        ],
        out_shape=[jax.ShapeDtypeStruct((m, n), F32), jax.ShapeDtypeStruct((m, n), h_dtype)],
        scratch_shapes=[pltpu.VMEM((tm, n), F32)],
        compiler_params=_params(("parallel", "arbitrary")),
        name="ffn_down",
    )(a, w, x, g)


def _pad_heads(v):
    return jnp.pad(v.astype(F32), (0, HEAD_PAD - N_HEADS)).reshape(1, HEAD_PAD)


def _row(v):
    return v.astype(F32).reshape(1, -1)


def kernel(x, norm_mix_g, w_in, conv_a_w, conv_a_b, ln_a_g, ln_a_b, w_conv_out, conv_b_w, conv_b_b, dt_bias, a_log, d_skip, ssm_norm_g, w_ssm_out, w_o, norm_ffn_g, w_gate, w_up, w_down, final_g):
    batch, seq, d = x.shape
    depth = w_in.shape[0]
    xf = x.reshape(batch * seq, d)
    h = _rmsnorm(xf, _row(norm_mix_g[0]))
    for l in range(depth):
        u = _proj_glu(h, w_in, l)
        zs = _proj_act(h, w_in, lambda j, l=l: (l, 0, COL_Z // 1024 + j), D_INNER, _silu, "proj_z")
        cwb = jnp.pad(conv_b_w[l].astype(F32), ((0, SUBLANES - K_CONV_B), (0, 0)))
        xbc = _proj_xbc(h, w_in, l, cwb, _row(conv_b_b[l]), seq)
        w_gates = w_in[l, :, COL_GATES:]
        gates = _proj_act(h, w_gates, lambda j: (0, j), 2 * D_MODEL, jax.nn.sigmoid, "proj_gates")
        w_dt = jnp.pad(w_in[l, :, COL_DT:COL_DT + N_HEADS],
                       ((0, 0), (0, HEAD_PAD - N_HEADS))).astype(BF16)
        cw = jnp.pad(conv_a_w[l].astype(F32), ((0, HIST_A - K_CONV_A), (0, 0)))
        ua = _branch_a(u, cw, _row(conv_a_b[l]), _row(ln_a_g[l]), _row(ln_a_b[l]), batch, seq)
        acol, rowp, wrow, alast = _dt_tables(h, w_dt, _pad_heads(dt_bias[l]), _pad_heads(a_log[l]))
        alast_exp = jnp.repeat(alast[:, :, :N_HEADS], HEAD_DIM, axis=2)
        dskip_exp = jnp.repeat(d_skip[l].astype(F32), HEAD_DIM).reshape(1, D_INNER)
        y = _ssd(xbc, acol, rowp, wrow, alast_exp, dskip_exp, batch, seq)
        merged = _merge(y, zs, _row(ssm_norm_g[l]), ua, gates, w_ssm_out[l].astype(BF16),
                        w_conv_out[l].astype(BF16))
        xf, h = _wo(merged, w_o, l, xf, _row(norm_ffn_g[l]))
        act = _ffn_up(h, w_gate, w_up, l)
        last = l == depth - 1
        g_next = final_g if last else norm_mix_g[l + 1]
        xf, h = _ffn_down(act, w_down[l].astype(BF16), xf, _row(g_next), F32 if last else BF16)
    return h.reshape(batch, seq, d)
```

```python
import functools

import jax
import jax.numpy as jnp
from jax import lax
from jax.experimental import pallas as pl
from jax.experimental.pallas import tpu as pltpu

F32 = jnp.float32
BF16 = jnp.bfloat16

D_MODEL = 2048
K_CONV_A = 31
D_INNER = 2 * D_MODEL
HEAD_DIM = 64
N_HEADS = D_INNER // HEAD_DIM
N_GROUPS = 8
HEADS_PER_GROUP = N_HEADS // N_GROUPS
D_STATE = 128
K_CONV_B = 4
BC_DIM = N_GROUPS * D_STATE
CONV_B_DIM = D_INNER + 2 * BC_DIM
CHUNK = 128
EPS = 1e-6

COL_GLU = 0
COL_Z = 2 * D_MODEL
COL_XBC = COL_Z + D_INNER
COL_DT = COL_XBC + CONV_B_DIM
COL_GATES = COL_DT + N_HEADS

LANES = 128
SUBLANES = 8
HEAD_PAD = LANES
VMEM_LIMIT = 56 * 1024 * 1024


def _params(sem):
    return pltpu.CompilerParams(dimension_semantics=sem, vmem_limit_bytes=VMEM_LIMIT)


def _rows_loop(n_rows, blk, fn, unroll=1):
    def body(i, carry):
        fn(pl.multiple_of(i * blk, blk))
        return carry
    lax.fori_loop(0, n_rows // blk, body, 0, unroll=unroll)


def _cast_rows(src_ref, dst_ref, col0=0):
    w = src_ref.shape[1]
    def blk(r0):
        dst_ref[pl.ds(r0, 32), col0:col0 + w] = src_ref[pl.ds(r0, 32), :].astype(BF16)
    _rows_loop(src_ref.shape[0], 32, blk)


def _rmsnorm_rows(x_ref, g_ref, h_ref, n_rows):
    def blk(r0):
        x = x_ref[pl.ds(r0, 16), :]
        ms = jnp.mean(x * x, axis=-1, keepdims=True)
        h_ref[pl.ds(r0, 16), :] = (x * lax.rsqrt(ms + EPS) * g_ref[...]).astype(h_ref.dtype)
    _rows_loop(n_rows, 16, blk, unroll=2)


def _silu(v):
    return v * jax.nn.sigmoid(v)


def _rmsnorm_kernel(x_ref, g_ref, h_ref):
    _rmsnorm_rows(x_ref, g_ref, h_ref, x_ref.shape[0])


def _rmsnorm(x, g, tm=512):
    m, d = x.shape
    return pl.pallas_call(
        _rmsnorm_kernel,
        grid=(m // tm,),
        in_specs=[pl.BlockSpec((tm, d), lambda i: (i, 0)), pl.BlockSpec((1, d), lambda i: (0, 0))],
        out_specs=pl.BlockSpec((tm, d), lambda i: (i, 0)),
        out_shape=jax.ShapeDtypeStruct((m, d), BF16),
        compiler_params=_params(("parallel",)),
        name="rmsnorm",
    )(x, g)


def _proj_glu_kernel(h_ref, wa_ref, wb_ref, o_ref, wbf_ref):
    @pl.when(pl.program_id(1) == 0)
    def _():
        _cast_rows(wa_ref, wbf_ref, 0)
        _cast_rows(wb_ref, wbf_ref, wa_ref.shape[1])

    tn = o_ref.shape[1]
    r = jnp.dot(h_ref[...], wbf_ref[...], preferred_element_type=F32)
    o_ref[...] = (r[:, :tn] * jax.nn.sigmoid(r[:, tn:])).astype(o_ref.dtype)


def _proj_glu(h, w_in, layer, tm=1024, tn=512):
    m, d = h.shape
    nb = D_MODEL // tn
    return pl.pallas_call(
        _proj_glu_kernel,
        grid=(nb, m // tm),
        in_specs=[
            pl.BlockSpec((tm, d), lambda j, i: (i, 0)),
            pl.BlockSpec((None, d, tn), lambda j, i: (layer, 0, COL_GLU // tn + j)),
            pl.BlockSpec((None, d, tn), lambda j, i: (layer, 0, COL_GLU // tn + nb + j)),
        ],
        out_specs=pl.BlockSpec((tm, tn), lambda j, i: (i, j)),
        out_shape=jax.ShapeDtypeStruct((m, D_MODEL), BF16),
        scratch_shapes=[pltpu.VMEM((d, 2 * tn), BF16)],
        compiler_params=_params(("parallel", "arbitrary")),
        name="proj_glu",
    )(h, w_in, w_in)


HIST_A = 32
TT_A = 128
N_CHUNK = 256
ROWS_CHUNK = 32
WIN_A = 64
LANE_BLK_A = 256


def _conv_a_rows(s_ref, cw8_ref, cb_ref, cv_ref, r0):
    base = HIST_A - (K_CONV_A - 1)
    nq = ROWS_CHUNK // SUBLANES
    for c0 in range(0, D_MODEL, LANE_BLK_A):
        win = s_ref[pl.ds(r0, WIN_A), c0:c0 + LANE_BLK_A]
        bias = jnp.broadcast_to(cb_ref[:, c0:c0 + LANE_BLK_A], (SUBLANES, LANE_BLK_A))
        acc = [bias] * nq
        for b in range(SUBLANES):
            wb = win if b == 0 else pltpu.roll(win, WIN_A - b, 0)
            n_valid = WIN_A if b == 0 else WIN_A - SUBLANES
            for a in range(WIN_A // SUBLANES):
                k = a * SUBLANES + b - base
                if k < 0 or k >= K_CONV_A or (a + nq) * SUBLANES > n_valid:
                    continue
                w = cw8_ref[k * SUBLANES:(k + 1) * SUBLANES, c0:c0 + LANE_BLK_A]
                for q in range(nq):
                    acc[q] = acc[q] + wb[(a + q) * SUBLANES:(a + q + 1) * SUBLANES, :] * w
        for q in range(nq):
            cv_ref[q * SUBLANES:(q + 1) * SUBLANES, c0:c0 + LANE_BLK_A] = acc[q]


def _ln_silu_rows(cv_ref, g_ref, b_ref, o_ref, r0):
    for q in range(ROWS_CHUNK // 16):
        v = cv_ref[q * 16:(q + 1) * 16, :]
        mu = jnp.mean(v, axis=-1, keepdims=True)
        vc = v - mu
        var = jnp.mean(vc * vc, axis=-1, keepdims=True)
        y = vc * lax.rsqrt(var + EPS) * g_ref[...] + b_ref[...]
        o_ref[pl.ds(r0 + q * 16, 16), :] = _silu(y).astype(o_ref.dtype)


def _proj_zg_kernel(h_ref, wz_ref, wga_ref, wgb_ref, u_ref, cw8_ref, cb_ref, lng_ref, lnb_ref,
                    o_ref, ua_ref, wbf_ref, s_ref, cv_ref, *, n_z, tiles_per_seq):
    j = pl.program_id(0)
    i = pl.program_id(1)
    d, tn = wbf_ref.shape

    @pl.when((i == 0) & (j < n_z))
    def _():
        _cast_rows(wz_ref, wbf_ref)

    @pl.when((i == 0) & (j >= n_z))
    def _():
        half = LANES // 2
        lane = lax.broadcasted_iota(jnp.int32, (32, LANES), 1)

        def blk(r0):
            a = pltpu.roll(wga_ref[pl.ds(r0, 32), :], tn - half, 1)
            b = pltpu.roll(wgb_ref[pl.ds(r0, 32), :], LANES - half, 1)
            wbf_ref[pl.ds(r0, 32), 0:tn - LANES] = a[:, :tn - LANES].astype(BF16)
            tail = jnp.where(lane < half, a[:, tn - LANES:], b)
            wbf_ref[pl.ds(r0, 32), tn - LANES:tn] = tail.astype(BF16)
        _rows_loop(d, 32, blk)

    t = j * pl.num_programs(1) + i

    @pl.when(t % tiles_per_seq == 0)
    def _():
        s_ref[0:HIST_A, :] = jnp.zeros((HIST_A, s_ref.shape[1]), F32)

    @pl.when(t % tiles_per_seq != 0)
    def _():
        s_ref[0:HIST_A, :] = s_ref[TT_A:TT_A + HIST_A, :]

    is_z = j < n_z

    def chunk(c, carry):
        n0 = pl.multiple_of(c * N_CHUNK, N_CHUNK)
        r = jnp.dot(h_ref[...], wbf_ref[:, pl.ds(n0, N_CHUNK)], preferred_element_type=F32)
        sg = jax.nn.sigmoid(r)
        o_ref[:, pl.ds(n0, N_CHUNK)] = jnp.where(is_z, r * sg, sg).astype(o_ref.dtype)
        r0 = pl.multiple_of(c * ROWS_CHUNK, ROWS_CHUNK)
        s_ref[pl.ds(HIST_A + r0, ROWS_CHUNK), :] = u_ref[pl.ds(r0, ROWS_CHUNK), :].astype(F32)
        _conv_a_rows(s_ref, cw8_ref, cb_ref, cv_ref, r0)
        _ln_silu_rows(cv_ref, lng_ref, lnb_ref, ua_ref, r0)
        return carry
    lax.fori_loop(0, tn // N_CHUNK, chunk, 0)


def _proj_zg(h, w_in, layer, u, cw8, conv_b, ln_g, ln_b, seq, tm=1024, tn=1024):
    m, d = h.shape
    n_z = D_INNER // tn
    n_g = 2 * D_MODEL // tn
    n_i = m // tm
    assert (n_z + n_g) * n_i * TT_A == m and tn // N_CHUNK * ROWS_CHUNK == TT_A
    gate_tile0 = (COL_GATES - LANES // 2) // tn
    one = pl.Buffered(1)
    row = pl.BlockSpec((1, D_MODEL), lambda j, i: (0, 0))

    def gate_tile(j):
        return gate_tile0 + jnp.maximum(j - n_z, 0)

    return pl.pallas_call(
        functools.partial(_proj_zg_kernel, n_z=n_z, tiles_per_seq=seq // TT_A),
        grid=(n_z + n_g, n_i),
        in_specs=[
            pl.BlockSpec((tm, d), lambda j, i: (i, 0)),
            pl.BlockSpec((None, d, tn),
                         lambda j, i: (layer, 0, COL_Z // tn + jnp.minimum(j, n_z - 1)),
                         pipeline_mode=one),
            pl.BlockSpec((None, d, tn), lambda j, i: (layer, 0, gate_tile(j)), pipeline_mode=one),
            pl.BlockSpec((None, d, LANES),
                         lambda j, i: (layer, 0, (gate_tile(j) + 1) * (tn // LANES)),
                         pipeline_mode=one),
            pl.BlockSpec((TT_A, D_MODEL), lambda j, i: (j * n_i + i, 0)),
            pl.BlockSpec((K_CONV_A * SUBLANES, D_MODEL), lambda j, i: (0, 0)),
            row, row, row,
        ],
        out_specs=[
            pl.BlockSpec((tm, tn), lambda j, i: (i, j)),
            pl.BlockSpec((TT_A, D_MODEL), lambda j, i: (j * n_i + i, 0)),
        ],
        out_shape=[
            jax.ShapeDtypeStruct((m, (n_z + n_g) * tn), BF16),
            jax.ShapeDtypeStruct((m, D_MODEL), BF16),
        ],
        scratch_shapes=[
            pltpu.VMEM((d, tn), BF16),
            pltpu.VMEM((TT_A + HIST_A, D_MODEL), F32),
            pltpu.VMEM((ROWS_CHUNK, D_MODEL), F32),
        ],
        compiler_params=_params(("arbitrary", "arbitrary")),
        name="proj_zg",
    )(h, w_in, w_in, w_in, u, cw8, conv_b, ln_g, ln_b)


def _proj_xbc_kernel(h_ref, w_ref, cw_ref, cb_ref, o_ref, wbf_ref, hist_ref, *, tiles_per_seq):
    i = pl.program_id(1)

    @pl.when(i == 0)
    def _():
        _cast_rows(w_ref, wbf_ref)

    @pl.when(i % tiles_per_seq == 0)
    def _():
        hist_ref[...] = jnp.zeros(hist_ref.shape, F32)

    tm = o_ref.shape[0]
    r = jnp.dot(h_ref[...], wbf_ref[...], preferred_element_type=F32)
    ext = jnp.concatenate([hist_ref[...], r], axis=0)
    acc = cb_ref[...] + r * cw_ref[K_CONV_B - 1:K_CONV_B, :]
    for k in range(K_CONV_B - 1):
        off = SUBLANES - (K_CONV_B - 1) + k
        acc = acc + ext[off:off + tm, :] * cw_ref[k:k + 1, :]
    o_ref[...] = _silu(acc).astype(o_ref.dtype)
    hist_ref[...] = r[tm - SUBLANES:, :]


def _proj_xbc(h, w_in, layer, conv_w, conv_b, seq, tm=1024, tn=1024):
    m, d = h.shape
    return pl.pallas_call(
        functools.partial(_proj_xbc_kernel, tiles_per_seq=seq // tm),
        grid=(CONV_B_DIM // tn, m // tm),
        in_specs=[
            pl.BlockSpec((tm, d), lambda j, i: (i, 0)),
            pl.BlockSpec((None, d, tn), lambda j, i: (layer, 0, COL_XBC // tn + j)),
            pl.BlockSpec((SUBLANES, tn), lambda j, i: (0, j)),
            pl.BlockSpec((1, tn), lambda j, i: (0, j)),
        ],
        out_specs=pl.BlockSpec((tm, tn), lambda j, i: (i, j)),
        out_shape=jax.ShapeDtypeStruct((m, CONV_B_DIM), BF16),
        scratch_shapes=[pltpu.VMEM((d, tn), BF16), pltpu.VMEM((SUBLANES, tn), F32)],
        compiler_params=_params(("parallel", "arbitrary")),
        name="proj_xbc",
    )(h, w_in, conv_w, conv_b)


def _split3(v):
    p1 = v.astype(BF16)
    r1 = v - p1.astype(F32)
    p2 = r1.astype(BF16)
    p3 = (r1 - p2.astype(F32)).astype(BF16)
    return p1, p2, p3


def _dt_kernel(h_ref, wdt_ref, bias_ref, alog_ref, acol_ref, rowp_ref, wrow_ref, alast_ref):
    x = jnp.dot(h_ref[...], wdt_ref[...], preferred_element_type=F32) + bias_ref[...]
    dt = jnp.maximum(x, 0.0) + jnp.log1p(jnp.exp(-jnp.abs(x)))
    adt = dt * (-jnp.exp(alog_ref[...]))
    li = lax.broadcasted_iota(jnp.int32, (CHUNK, CHUNK), 0)
    si = lax.broadcasted_iota(jnp.int32, (CHUNK, CHUNK), 1)
    tril = jnp.where(li >= si, 1.0, 0.0).astype(BF16)
    acs = None
    for p in _split3(adt):
        term = jnp.dot(tril, p, preferred_element_type=F32)
        acs = term if acs is None else acs + term
    acol_ref[...] = acs
    alast = acs[CHUNK - 1:CHUNK, :]
    alast_ref[0] = jnp.broadcast_to(alast, (SUBLANES, HEAD_PAD))
    rowp_ref[0] = (acs - jnp.log(dt)).T
    wrow_ref[0] = (jnp.exp(alast - acs) * dt).T


def _dt_tables(h, w_dt, dt_bias, a_log):
    m, d = h.shape
    nchunks = m // CHUNK
    row = pl.BlockSpec((1, HEAD_PAD), lambda i: (0, 0))
    tab = pl.BlockSpec((1, HEAD_PAD, CHUNK), lambda i: (i, 0, 0))
    return pl.pallas_call(
        _dt_kernel,
        grid=(nchunks,),
        in_specs=[
            pl.BlockSpec((CHUNK, d), lambda i: (i, 0)),
            pl.BlockSpec((d, HEAD_PAD), lambda i: (0, 0)),
            row,
            row,
        ],
        out_specs=[
            pl.BlockSpec((CHUNK, HEAD_PAD), lambda i: (i, 0)),
            tab,
            tab,
            pl.BlockSpec((1, SUBLANES, HEAD_PAD), lambda i: (i, 0, 0)),
        ],
        out_shape=[
            jax.ShapeDtypeStruct((m, HEAD_PAD), F32),
            jax.ShapeDtypeStruct((nchunks, HEAD_PAD, CHUNK), F32),
            jax.ShapeDtypeStruct((nchunks, HEAD_PAD, CHUNK), F32),
            jax.ShapeDtypeStruct((nchunks, SUBLANES, HEAD_PAD), F32),
        ],
        compiler_params=_params(("parallel",)),
        name="dt_tables",
    )(h, w_dt, dt_bias, a_log)


def _ssd_kernel(xbc_ref, acol_ref, rowp_ref, wrow_ref, alast_ref, dskip_ref, zs_ref, ng_ref,
                v_ref, st_ref, vt_ref):
    @pl.when(pl.program_id(1) == 0)
    def _():
        st_ref[...] = jnp.zeros(st_ref.shape, F32)

    li = lax.broadcasted_iota(jnp.int32, (CHUNK, CHUNK), 0)
    si = lax.broadcasted_iota(jnp.int32, (CHUNK, CHUNK), 1)
    causal = li >= si
    lo = si < HEAD_DIM
    gw = HEADS_PER_GROUP * HEAD_DIM
    ones = jnp.ones((2 * LANES, LANES), BF16)

    for g in range(N_GROUPS):
        bg = xbc_ref[:, D_INNER + g * D_STATE:D_INNER + (g + 1) * D_STATE]
        cg = xbc_ref[:, D_INNER + BC_DIM + g * D_STATE:D_INNER + BC_DIM + (g + 1) * D_STATE]
        cb = lax.dot_general(cg, bg, (((1,), (1,)), ((), ())), preferred_element_type=F32)
        bgt = bg.astype(F32).T
        s_old = st_ref[:, g * gw:(g + 1) * gw]
        y_off = jnp.dot(cg, s_old.astype(BF16), preferred_element_type=F32)
        decay = jnp.exp(alast_ref[0, 0:1, g * gw:(g + 1) * gw])
        ssq = jnp.zeros((CHUNK, LANES), F32)
        for j in range(HEADS_PER_GROUP // 2):
            c0 = g * gw + j * LANES
            xp = xbc_ref[:, c0:c0 + LANES].astype(F32)
            xbd = jnp.concatenate([jnp.where(lo, xp, 0.0), jnp.where(lo, 0.0, xp)],
                                  axis=0).astype(BF16)
            ms, ws, cols = [], [], []
            for h in (g * HEADS_PER_GROUP + 2 * j, g * HEADS_PER_GROUP + 2 * j + 1):
                colb = jnp.broadcast_to(acol_ref[:, h:h + 1], (CHUNK, CHUNK))
                seg = colb - rowp_ref[0, h:h + 1, :]
                lmat = jnp.exp(jnp.where(causal, seg, -jnp.inf))
                ms.append((cb * lmat).astype(BF16))
                ws.append((bgt * wrow_ref[0, h:h + 1, :]).astype(BF16))
                cols.append(colb)
            y_diag = jnp.dot(jnp.concatenate(ms, axis=1), xbd, preferred_element_type=F32)
            s_new = jnp.dot(jnp.concatenate(ws, axis=1), xbd, preferred_element_type=F32)
            e = jnp.exp(jnp.where(lo, cols[0], cols[1]))
            jl = slice(j * LANES, (j + 1) * LANES)
            y = y_diag + y_off[:, jl] * e + dskip_ref[:, c0:c0 + LANES] * xp
            st_ref[:, c0:c0 + LANES] = s_old[:, jl] * decay[:, jl] + s_new
            v = y * zs_ref[:, c0:c0 + LANES].astype(F32)
            ssq = ssq + v * v
            vt_ref[:, jl] = v
        sq_hi = ssq.astype(BF16)
        sq_lo = (ssq - sq_hi.astype(F32)).astype(BF16)
        tot = jnp.dot(jnp.concatenate([sq_hi, sq_lo], axis=1), ones, preferred_element_type=F32)
        scale = lax.rsqrt(tot * (1.0 / gw) + EPS)
        for j in range(HEADS_PER_GROUP // 2):
            c0 = g * gw + j * LANES
            jl = slice(j * LANES, (j + 1) * LANES)
            v_ref[:, c0:c0 + LANES] = (
                vt_ref[:, jl] * scale * ng_ref[:, c0:c0 + LANES]).astype(v_ref.dtype)


def _ssd(xbc, acol, rowp, wrow, alast_exp, dskip_exp, zg, norm_g, batch, seq):
    m = xbc.shape[0]
    nc = seq // CHUNK
    tab = pl.BlockSpec((1, HEAD_PAD, CHUNK), lambda b, c: (b * nc + c, 0, 0))
    return pl.pallas_call(
        _ssd_kernel,
        grid=(batch, nc),
        in_specs=[
            pl.BlockSpec((CHUNK, CONV_B_DIM), lambda b, c: (b * nc + c, 0)),
            pl.BlockSpec((CHUNK, HEAD_PAD), lambda b, c: (b * nc + c, 0)),
            tab,
            tab,
            pl.BlockSpec((1, SUBLANES, D_INNER), lambda b, c: (b * nc + c, 0, 0)),
            pl.BlockSpec((1, D_INNER), lambda b, c: (0, 0)),
            pl.BlockSpec((CHUNK, D_INNER), lambda b, c: (b * nc + c, 0)),
            pl.BlockSpec((1, D_INNER), lambda b, c: (0, 0)),
        ],
        out_specs=pl.BlockSpec((CHUNK, D_INNER), lambda b, c: (b * nc + c, 0)),
        out_shape=jax.ShapeDtypeStruct((m, D_INNER), BF16),
        scratch_shapes=[
            pltpu.VMEM((D_STATE, D_INNER), F32),
            pltpu.VMEM((CHUNK, HEADS_PER_GROUP * HEAD_DIM), F32),
        ],
        compiler_params=_params(("parallel", "arbitrary")),
        name="ssd",
    )(xbc, acol, rowp, wrow, alast_exp, dskip_exp, zg, norm_g)


def _merge_kernel(v_ref, u_ref, ga_ref, gb_ref, wss_ref, wco_ref, o_ref):
    y_b = jnp.dot(v_ref[...], wss_ref[...], preferred_element_type=F32)
    y_a = jnp.dot(u_ref[...], wco_ref[...], preferred_element_type=F32)
    o_ref[...] = (ga_ref[...].astype(F32) * y_a + gb_ref[...].astype(F32) * y_b).astype(o_ref.dtype)


def _merge(v, zg, u, w_ssm_out, w_conv_out, tm=1024, tn=512):
    m = v.shape[0]
    n = D_MODEL
    return pl.pallas_call(
        _merge_kernel,
        grid=(m // tm, n // tn),
        in_specs=[
            pl.BlockSpec((tm, D_INNER), lambda i, j: (i, 0)),
            pl.BlockSpec((tm, D_MODEL), lambda i, j: (i, 0)),
            pl.BlockSpec((tm, tn), lambda i, j: (i, D_INNER // tn + j)),
            pl.BlockSpec((tm, tn), lambda i, j: (i, (D_INNER + n) // tn + j)),
            pl.BlockSpec((D_INNER, tn), lambda i, j: (0, j)),
            pl.BlockSpec((D_MODEL, tn), lambda i, j: (0, j)),
        ],
        out_specs=pl.BlockSpec((tm, tn), lambda i, j: (i, j)),
        out_shape=jax.ShapeDtypeStruct((m, n), BF16),
        compiler_params=_params(("parallel", "arbitrary")),
        name="merge",
    )(v, u, zg, zg, w_ssm_out, w_conv_out)


def _wo_kernel(a_ref, w_ref, x_ref, g_ref, xo_ref, h_ref, wbf_ref):
    @pl.when(pl.program_id(0) == 0)
    def _():
        _cast_rows(w_ref, wbf_ref)

    xo_ref[...] = x_ref[...] + jnp.dot(a_ref[...], wbf_ref[...], preferred_element_type=F32)
    _rmsnorm_rows(xo_ref, g_ref, h_ref, xo_ref.shape[0])


def _wo(a, w_o, layer, x, g, tm=256):
    m, k = a.shape
    n = D_MODEL
    return pl.pallas_call(
        _wo_kernel,
        grid=(m // tm,),
        in_specs=[
            pl.BlockSpec((tm, k), lambda i: (i, 0)),
            pl.BlockSpec((None, k, n), lambda i: (layer, 0, 0), pipeline_mode=pl.Buffered(1)),
            pl.BlockSpec((tm, n), lambda i: (i, 0)),
            pl.BlockSpec((1, n), lambda i: (0, 0)),
        ],
        out_specs=[
            pl.BlockSpec((tm, n), lambda i: (i, 0)),
            pl.BlockSpec((tm, n), lambda i: (i, 0)),
        ],
        out_shape=[jax.ShapeDtypeStruct((m, n), F32), jax.ShapeDtypeStruct((m, n), BF16)],
        scratch_shapes=[pltpu.VMEM((k, n), BF16)],
        compiler_params=_params(("arbitrary",)),
        name="wo",
    )(a, w_o, x, g)


def _ffn_up_kernel(h_ref, wg_ref, wu_ref, o_ref, wbf_ref):
    @pl.when(pl.program_id(1) == 0)
    def _():
        _cast_rows(wg_ref, wbf_ref, 0)
        _cast_rows(wu_ref, wbf_ref, wg_ref.shape[1])

    tn = o_ref.shape[1]
    r = jnp.dot(h_ref[...], wbf_ref[...], preferred_element_type=F32)
    o_ref[...] = (_silu(r[:, :tn]) * r[:, tn:]).astype(o_ref.dtype)


def _ffn_up(h, w_gate, w_up, layer, tm=1024, tn=512):
    m, d = h.shape
    n = w_gate.shape[2]
    wspec = pl.BlockSpec((None, d, tn), lambda j, i: (layer, 0, j))
    return pl.pallas_call(
        _ffn_up_kernel,
        grid=(n // tn, m // tm),
        in_specs=[pl.BlockSpec((tm, d), lambda j, i: (i, 0)), wspec, wspec],
        out_specs=pl.BlockSpec((tm, tn), lambda j, i: (i, j)),
        out_shape=jax.ShapeDtypeStruct((m, n), BF16),
        scratch_shapes=[pltpu.VMEM((d, 2 * tn), BF16)],
        compiler_params=_params(("parallel", "arbitrary")),
        name="ffn_up",
    )(h, w_gate, w_up)


def _ffn_down_kernel(a_ref, w_ref, x_ref, g_ref, *refs, keep_residual):
    acc_ref, h_ref = refs if keep_residual else refs[::-1]
    k = pl.program_id(1)

    @pl.when(k == 0)
    def _():
        acc_ref[...] = x_ref[...]

    acc_ref[...] += jnp.dot(a_ref[...], w_ref[...], preferred_element_type=F32)

    @pl.when(k == pl.num_programs(1) - 1)
    def _():
        _rmsnorm_rows(acc_ref, g_ref, h_ref, acc_ref.shape[0])


def _ffn_down(a, w, x, g, keep_residual, tm=1024, tk=512):
    m, kdim = a.shape
    n = w.shape[1]
    blk = pl.BlockSpec((tm, n), lambda i, k: (i, 0))
    if keep_residual:
        out_specs = [blk, blk]
        out_shape = [jax.ShapeDtypeStruct((m, n), F32), jax.ShapeDtypeStruct((m, n), BF16)]
        scratch = []
    else:
        out_specs = [blk]
        out_shape = [jax.ShapeDtypeStruct((m, n), F32)]
        scratch = [pltpu.VMEM((tm, n), F32)]
    return pl.pallas_call(
        functools.partial(_ffn_down_kernel, keep_residual=keep_residual),
        grid=(m // tm, kdim // tk),
        in_specs=[
            pl.BlockSpec((tm, tk), lambda i, k: (i, k)),
            pl.BlockSpec((tk, n), lambda i, k: (k, 0)),
            pl.BlockSpec((tm, n), lambda i, k: (i, 0), pipeline_mode=pl.Buffered(1)),
            pl.BlockSpec((1, n), lambda i, k: (0, 0)),
        ],
        out_specs=out_specs,
        out_shape=out_shape,
        scratch_shapes=scratch,
        compiler_params=_params(("parallel", "arbitrary")),
        name="ffn_down",
    )(a, w, x, g)


def _pad_heads(v):
    return jnp.pad(v.astype(F32), (0, HEAD_PAD - N_HEADS)).reshape(1, HEAD_PAD)


def _row(v):
    return v.astype(F32).reshape(1, -1)


def kernel(x, norm_mix_g, w_in, conv_a_w, conv_a_b, ln_a_g, ln_a_b, w_conv_out, conv_b_w, conv_b_b, dt_bias, a_log, d_skip, ssm_norm_g, w_ssm_out, w_o, norm_ffn_g, w_gate, w_up, w_down, final_g):
    batch, seq, d = x.shape
    depth = w_in.shape[0]
    xf = x.reshape(batch * seq, d)
    h = _rmsnorm(xf, _row(norm_mix_g[0]))
    for l in range(depth):
        u = _proj_glu(h, w_in, l)
        cwb = jnp.pad(conv_b_w[l].astype(F32), ((0, SUBLANES - K_CONV_B), (0, 0)))
        xbc = _proj_xbc(h, w_in, l, cwb, _row(conv_b_b[l]), seq)
        cw8 = jnp.repeat(conv_a_w[l].astype(F32), SUBLANES, axis=0)
        zg, ua = _proj_zg(h, w_in, l, u, cw8, _row(conv_a_b[l]), _row(ln_a_g[l]),
                          _row(ln_a_b[l]), seq)
        w_dt = jnp.pad(w_in[l, :, COL_DT:COL_DT + N_HEADS],
                       ((0, 0), (0, HEAD_PAD - N_HEADS))).astype(BF16)
        acol, rowp, wrow, alast = _dt_tables(h, w_dt, _pad_heads(dt_bias[l]), _pad_heads(a_log[l]))
        alast_exp = jnp.repeat(alast[:, :, :N_HEADS], HEAD_DIM, axis=2)
        dskip_exp = jnp.repeat(d_skip[l].astype(F32), HEAD_DIM).reshape(1, D_INNER)
        v = _ssd(xbc, acol, rowp, wrow, alast_exp, dskip_exp, zg, _row(ssm_norm_g[l]), batch, seq)
        merged = _merge(v, zg, ua, w_ssm_out[l].astype(BF16), w_conv_out[l].astype(BF16))
        xf, h = _wo(merged, w_o, l, xf, _row(norm_ffn_g[l]))
        act = _ffn_up(h, w_gate, w_up, l)
        w_dn = w_down[l].astype(BF16)
        if l < depth - 1:
            xf, h = _ffn_down(act, w_dn, xf, _row(norm_mix_g[l + 1]), keep_residual=True)
        else:
            (h,) = _ffn_down(act, w_dn, xf, _row(final_g), keep_residual=False)
    return h.reshape(batch, seq, d)
```

```python
import functools

import jax
import jax.numpy as jnp
from jax import lax
from jax.experimental import pallas as pl
from jax.experimental.pallas import tpu as pltpu

F32 = jnp.float32
BF16 = jnp.bfloat16

D_MODEL = 2048
K_CONV_A = 31
D_INNER = 2 * D_MODEL
HEAD_DIM = 64
N_HEADS = D_INNER // HEAD_DIM
N_GROUPS = 8
HEADS_PER_GROUP = N_HEADS // N_GROUPS
D_STATE = 128
K_CONV_B = 4
BC_DIM = N_GROUPS * D_STATE
CONV_B_DIM = D_INNER + 2 * BC_DIM
CHUNK = 128
EPS = 1e-6
LOG2_E = 1.4426950408889634

COL_GLU = 0
COL_Z = 2 * D_MODEL
COL_XBC = COL_Z + D_INNER
COL_DT = COL_XBC + CONV_B_DIM
COL_GATES = COL_DT + N_HEADS
GATE_SKEW = N_HEADS

LANES = 128
SUBLANES = 8
HEAD_PAD = LANES
VMEM_LIMIT = 56 * 1024 * 1024


def _params(sem):
    return pltpu.CompilerParams(dimension_semantics=sem, vmem_limit_bytes=VMEM_LIMIT)


def _rows_loop(n_rows, blk, fn, unroll=1):
    def body(i, carry):
        fn(pl.multiple_of(i * blk, blk))
        return carry
    lax.fori_loop(0, n_rows // blk, body, 0, unroll=unroll)


def _cast_rows(src_ref, dst_ref, col0=0):
    w = src_ref.shape[1]
    def blk(r0):
        dst_ref[pl.ds(r0, 32), col0:col0 + w] = src_ref[pl.ds(r0, 32), :].astype(BF16)
    _rows_loop(src_ref.shape[0], 32, blk)


T_BLK = 256


def _cast_transposed(get_block, n_rows, dst_ref, col0=0):
    for n0 in range(0, n_rows, T_BLK):
        def body(kb, carry, n0=n0):
            k0 = pl.multiple_of(kb * T_BLK, T_BLK)
            dst_ref[pl.ds(k0, T_BLK), col0 + n0:col0 + n0 + T_BLK] = (
                get_block(n0, k0).T.astype(BF16))
            return carry
        lax.fori_loop(0, dst_ref.shape[0] // T_BLK, body, 0)


def _block_of(ref):
    return lambda n0, k0: ref[n0:n0 + T_BLK, pl.ds(k0, T_BLK)]


def _rmsnorm_rows(x_ref, g_ref, h_ref, n_rows):
    def blk(r0):
        x = x_ref[pl.ds(r0, 16), :]
        ms = jnp.mean(x * x, axis=-1, keepdims=True)
        h_ref[pl.ds(r0, 16), :] = (x * lax.rsqrt(ms + EPS) * g_ref[...]).astype(h_ref.dtype)
    _rows_loop(n_rows, 16, blk, unroll=2)


def _silu(v):
    return v * jax.nn.sigmoid(v)


def _rmsnorm_kernel(x_ref, g_ref, h_ref):
    _rmsnorm_rows(x_ref, g_ref, h_ref, x_ref.shape[0])


def _rmsnorm(x, g, tm=512):
    m, d = x.shape
    return pl.pallas_call(
        _rmsnorm_kernel,
        grid=(m // tm,),
        in_specs=[pl.BlockSpec((tm, d), lambda i: (i, 0)), pl.BlockSpec((1, d), lambda i: (0, 0))],
        out_specs=pl.BlockSpec((tm, d), lambda i: (i, 0)),
        out_shape=jax.ShapeDtypeStruct((m, d), BF16),
        compiler_params=_params(("parallel",)),
        name="rmsnorm",
    )(x, g)


def _proj_glu_kernel(h_ref, wa_ref, wb_ref, o_ref, wbf_ref):
    tn = o_ref.shape[1]

    @pl.when(pl.program_id(1) == 0)
    def _():
        _cast_transposed(_block_of(wa_ref), tn, wbf_ref, 0)
        _cast_transposed(_block_of(wb_ref), tn, wbf_ref, tn)

    r = jnp.dot(h_ref[...], wbf_ref[...], preferred_element_type=F32)
    o_ref[...] = (r[:, :tn] * jax.nn.sigmoid(r[:, tn:])).astype(o_ref.dtype)


def _proj_glu(h, w_in_t, layer, tm=1024, tn=512):
    m, d = h.shape
    nb = D_MODEL // tn
    return pl.pallas_call(
        _proj_glu_kernel,
        grid=(nb, m // tm),
        in_specs=[
            pl.BlockSpec((tm, d), lambda j, i: (i, 0)),
            pl.BlockSpec((None, tn, d), lambda j, i: (layer, COL_GLU // tn + j, 0)),
            pl.BlockSpec((None, tn, d), lambda j, i: (layer, COL_GLU // tn + nb + j, 0)),
        ],
        out_specs=pl.BlockSpec((tm, tn), lambda j, i: (i, j)),
        out_shape=jax.ShapeDtypeStruct((m, D_MODEL), BF16),
        scratch_shapes=[pltpu.VMEM((d, 2 * tn), BF16)],
        compiler_params=_params(("parallel", "arbitrary")),
        name="proj_glu",
    )(h, w_in_t, w_in_t)


HIST_A = 32
TT_A = 128
N_CHUNK = 256
ROWS_CHUNK = 32
WIN_A = 64
LANE_BLK_A = 256


def _conv_a_rows(s_ref, cw8_ref, cb_ref, cv_ref, r0):
    base = HIST_A - (K_CONV_A - 1)
    nq = ROWS_CHUNK // SUBLANES
    for c0 in range(0, D_MODEL, LANE_BLK_A):
        win = s_ref[pl.ds(r0, WIN_A), c0:c0 + LANE_BLK_A]
        bias = jnp.broadcast_to(cb_ref[:, c0:c0 + LANE_BLK_A], (SUBLANES, LANE_BLK_A))
        acc = [bias] * nq
        for b in range(SUBLANES):
            wb = win if b == 0 else pltpu.roll(win, WIN_A - b, 0)
            n_valid = WIN_A if b == 0 else WIN_A - SUBLANES
            for a in range(WIN_A // SUBLANES):
                k = a * SUBLANES + b - base
                if k < 0 or k >= K_CONV_A or (a + nq) * SUBLANES > n_valid:
                    continue
                w = cw8_ref[k * SUBLANES:(k + 1) * SUBLANES, c0:c0 + LANE_BLK_A]
                for q in range(nq):
                    acc[q] = acc[q] + wb[(a + q) * SUBLANES:(a + q + 1) * SUBLANES, :] * w
        for q in range(nq):
            cv_ref[q * SUBLANES:(q + 1) * SUBLANES, c0:c0 + LANE_BLK_A] = acc[q]


def _ln_silu_rows(cv_ref, g_ref, b_ref, o_ref, r0):
    for q in range(ROWS_CHUNK // 16):
        v = cv_ref[q * 16:(q + 1) * 16, :]
        mu = jnp.mean(v, axis=-1, keepdims=True)
        vc = v - mu
        var = jnp.mean(vc * vc, axis=-1, keepdims=True)
        y = vc * lax.rsqrt(var + EPS) * g_ref[...] + b_ref[...]
        o_ref[pl.ds(r0 + q * 16, 16), :] = _silu(y).astype(o_ref.dtype)


def _proj_zg_kernel(h_ref, wz_ref, wga_ref, wgb_ref, u_ref, cw8_ref, cb_ref, lng_ref, lnb_ref,
                    o_ref, ua_ref, wbf_ref, s_ref, cv_ref, *, n_z, tiles_per_seq):
    j = pl.program_id(0)
    i = pl.program_id(1)
    d, tn = wbf_ref.shape

    @pl.when((i == 0) & (j < n_z))
    def _():
        _cast_transposed(_block_of(wz_ref), tn, wbf_ref)

    @pl.when((i == 0) & (j >= n_z))
    def _():
        def gate_block(n0, k0):
            lo, hi = n0 + GATE_SKEW, n0 + GATE_SKEW + T_BLK
            if hi <= tn:
                return wga_ref[lo:hi, pl.ds(k0, T_BLK)]
            return jnp.concatenate([wga_ref[lo:tn, pl.ds(k0, T_BLK)],
                                    wgb_ref[0:hi - tn, pl.ds(k0, T_BLK)]], axis=0)
        _cast_transposed(gate_block, tn, wbf_ref)

    t = j * pl.num_programs(1) + i

    @pl.when(t % tiles_per_seq == 0)
    def _():
        s_ref[0:HIST_A, :] = jnp.zeros((HIST_A, s_ref.shape[1]), F32)

    @pl.when(t % tiles_per_seq != 0)
    def _():
        s_ref[0:HIST_A, :] = s_ref[TT_A:TT_A + HIST_A, :]

    is_z = j < n_z

    def chunk(c, carry):
        r0 = pl.multiple_of(c * ROWS_CHUNK, ROWS_CHUNK)
        s_ref[pl.ds(HIST_A + r0, ROWS_CHUNK), :] = u_ref[pl.ds(r0, ROWS_CHUNK), :].astype(F32)
        _conv_a_rows(s_ref, cw8_ref, cb_ref, cv_ref, r0)
        _ln_silu_rows(cv_ref, lng_ref, lnb_ref, ua_ref, r0)
        n0 = pl.multiple_of(c * N_CHUNK, N_CHUNK)
        r = jnp.dot(h_ref[...], wbf_ref[:, pl.ds(n0, N_CHUNK)], preferred_element_type=F32)
        sg = jax.nn.sigmoid(r)
        o_ref[:, pl.ds(n0, N_CHUNK)] = jnp.where(is_z, r * sg, sg).astype(o_ref.dtype)
        return carry
    lax.fori_loop(0, tn // N_CHUNK, chunk, 0)


def _proj_zg(h, w_in_t, layer, u, cw8, conv_b, ln_g, ln_b, seq, tm=1024, tn=1024):
    m, d = h.shape
    n_z = D_INNER // tn
    n_g = 2 * D_MODEL // tn
    n_i = m // tm
    assert (n_z + n_g) * n_i * TT_A == m and tn // N_CHUNK * ROWS_CHUNK == TT_A
    gate_tile0 = (COL_GATES - GATE_SKEW) // tn
    one = pl.Buffered(1)
    row = pl.BlockSpec((1, D_MODEL), lambda j, i: (0, 0))

    def gate_tile(j):
        return gate_tile0 + jnp.maximum(j - n_z, 0)

    return pl.pallas_call(
        functools.partial(_proj_zg_kernel, n_z=n_z, tiles_per_seq=seq // TT_A),
        grid=(n_z + n_g, n_i),
        in_specs=[
            pl.BlockSpec((tm, d), lambda j, i: (i, 0)),
            pl.BlockSpec((None, tn, d),
                         lambda j, i: (layer, COL_Z // tn + jnp.minimum(j, n_z - 1), 0),
                         pipeline_mode=one),
            pl.BlockSpec((None, tn, d), lambda j, i: (layer, gate_tile(j), 0), pipeline_mode=one),
            pl.BlockSpec((None, GATE_SKEW, d),
                         lambda j, i: (layer, (gate_tile(j) + 1) * (tn // GATE_SKEW), 0),
                         pipeline_mode=one),
            pl.BlockSpec((TT_A, D_MODEL), lambda j, i: (j * n_i + i, 0)),
            pl.BlockSpec((K_CONV_A * SUBLANES, D_MODEL), lambda j, i: (0, 0)),
            row, row, row,
        ],
        out_specs=[
            pl.BlockSpec((tm, tn), lambda j, i: (i, j)),
            pl.BlockSpec((TT_A, D_MODEL), lambda j, i: (j * n_i + i, 0)),
        ],
        out_shape=[
            jax.ShapeDtypeStruct((m, (n_z + n_g) * tn), BF16),
            jax.ShapeDtypeStruct((m, D_MODEL), BF16),
        ],
        scratch_shapes=[
            pltpu.VMEM((d, tn), BF16),
            pltpu.VMEM((TT_A + HIST_A, D_MODEL), F32),
            pltpu.VMEM((ROWS_CHUNK, D_MODEL), F32),
        ],
        compiler_params=_params(("arbitrary", "arbitrary")),
        name="proj_zg",
    )(h, w_in_t, w_in_t, w_in_t, u, cw8, conv_b, ln_g, ln_b)


def _proj_xbc_kernel(h_ref, w_ref, cw_ref, cb_ref, o_ref, wbf_ref, hist_ref, *, tiles_per_seq):
    i = pl.program_id(1)

    @pl.when(i == 0)
    def _():
        _cast_transposed(_block_of(w_ref), w_ref.shape[0], wbf_ref)

    @pl.when(i % tiles_per_seq == 0)
    def _():
        hist_ref[...] = jnp.zeros(hist_ref.shape, F32)

    tm = o_ref.shape[0]
    r = jnp.dot(h_ref[...], wbf_ref[...], preferred_element_type=F32)
    ext = jnp.concatenate([hist_ref[...], r], axis=0)
    acc = cb_ref[...] + r * cw_ref[K_CONV_B - 1:K_CONV_B, :]
    for k in range(K_CONV_B - 1):
        off = SUBLANES - (K_CONV_B - 1) + k
        acc = acc + ext[off:off + tm, :] * cw_ref[k:k + 1, :]
    o_ref[...] = _silu(acc).astype(o_ref.dtype)
    hist_ref[...] = r[tm - SUBLANES:, :]


def _proj_xbc(h, w_in_t, layer, conv_w, conv_b, seq, tm=1024, tn=1024):
    m, d = h.shape
    return pl.pallas_call(
        functools.partial(_proj_xbc_kernel, tiles_per_seq=seq // tm),
        grid=(CONV_B_DIM // tn, m // tm),
        in_specs=[
            pl.BlockSpec((tm, d), lambda j, i: (i, 0)),
            pl.BlockSpec((None, tn, d), lambda j, i: (layer, COL_XBC // tn + j, 0)),
            pl.BlockSpec((SUBLANES, tn), lambda j, i: (0, j)),
            pl.BlockSpec((1, tn), lambda j, i: (0, j)),
        ],
        out_specs=pl.BlockSpec((tm, tn), lambda j, i: (i, j)),
        out_shape=jax.ShapeDtypeStruct((m, CONV_B_DIM), BF16),
        scratch_shapes=[pltpu.VMEM((d, tn), BF16), pltpu.VMEM((SUBLANES, tn), F32)],
        compiler_params=_params(("parallel", "arbitrary")),
        name="proj_xbc",
    )(h, w_in_t, conv_w, conv_b)


def _split3(v):
    p1 = v.astype(BF16)
    r1 = v - p1.astype(F32)
    p2 = r1.astype(BF16)
    p3 = (r1 - p2.astype(F32)).astype(BF16)
    return p1, p2, p3


def _dt_kernel(h_ref, wdt_ref, bias_ref, alog_ref, acol_ref, rowp_ref, wrow_ref, alast_ref):
    x_all = lax.dot_general(h_ref[...], wdt_ref[...].astype(BF16), (((1,), (1,)), ((), ())),
                            preferred_element_type=F32) + bias_ref[...]
    a = -jnp.exp(alog_ref[...])
    li = lax.broadcasted_iota(jnp.int32, (CHUNK, CHUNK), 0)
    si = lax.broadcasted_iota(jnp.int32, (CHUNK, CHUNK), 1)
    tril = jnp.where(li >= si, 1.0, 0.0).astype(BF16)
    for q in range(h_ref.shape[0] // CHUNK):
        x = x_all[q * CHUNK:(q + 1) * CHUNK, :]
        dt = jnp.maximum(x, 0.0) + jnp.log1p(jnp.exp(-jnp.abs(x)))
        adt = dt * a
        acs = None
        for p in _split3(adt):
            term = jnp.dot(tril, p, preferred_element_type=F32)
            acs = term if acs is None else acs + term
        acol_ref[q * CHUNK:(q + 1) * CHUNK, :] = acs * LOG2_E
        alast = acs[CHUNK - 1:CHUNK, :]
        alast_ref[q] = jnp.broadcast_to(alast, (SUBLANES, HEAD_PAD))
        rowp_ref[q] = ((acs - jnp.log(dt)) * LOG2_E).T
        wrow_ref[q] = (jnp.exp(alast - acs) * dt).T


def _dt_tables(h, w_dt, dt_bias, a_log, chunks_per_step=4):
    m, d = h.shape
    nchunks = m // CHUNK
    cps = chunks_per_step
    row = pl.BlockSpec((1, HEAD_PAD), lambda i: (0, 0))
    tab = pl.BlockSpec((cps, HEAD_PAD, CHUNK), lambda i: (i, 0, 0))
    return pl.pallas_call(
        _dt_kernel,
        grid=(nchunks // cps,),
        in_specs=[
            pl.BlockSpec((cps * CHUNK, d), lambda i: (i, 0)),
            pl.BlockSpec((HEAD_PAD, d), lambda i: (0, 0)),
            row,
            row,
        ],
        out_specs=[
            pl.BlockSpec((cps * CHUNK, HEAD_PAD), lambda i: (i, 0)),
            tab,
            tab,
            pl.BlockSpec((cps, SUBLANES, HEAD_PAD), lambda i: (i, 0, 0)),
        ],
        out_shape=[
            jax.ShapeDtypeStruct((m, HEAD_PAD), F32),
            jax.ShapeDtypeStruct((nchunks, HEAD_PAD, CHUNK), F32),
            jax.ShapeDtypeStruct((nchunks, HEAD_PAD, CHUNK), F32),
            jax.ShapeDtypeStruct((nchunks, SUBLANES, HEAD_PAD), F32),
        ],
        compiler_params=_params(("parallel",)),
        name="dt_tables",
    )(h, w_dt, dt_bias, a_log)


def _ssd_kernel(xbc_ref, acol_ref, rowp_ref, wrow_ref, alast_ref, dskip_ref, zs_ref, ng_ref,
                v_ref, st_ref, vt_ref):
    @pl.when(pl.program_id(1) == 0)
    def _():
        st_ref[...] = jnp.zeros(st_ref.shape, F32)

    li = lax.broadcasted_iota(jnp.int32, (CHUNK, CHUNK), 0)
    si = lax.broadcasted_iota(jnp.int32, (CHUNK, CHUNK), 1)
    causal = li >= si
    lo = si < HEAD_DIM
    gw = HEADS_PER_GROUP * HEAD_DIM
    ones = jnp.ones((2 * LANES, LANES), BF16)

    for g in range(N_GROUPS):
        bg = xbc_ref[:, D_INNER + g * D_STATE:D_INNER + (g + 1) * D_STATE]
        cg = xbc_ref[:, D_INNER + BC_DIM + g * D_STATE:D_INNER + BC_DIM + (g + 1) * D_STATE]
        cb = lax.dot_general(cg, bg, (((1,), (1,)), ((), ())), preferred_element_type=F32)
        bgt = bg.astype(F32).T
        s_old = st_ref[:, g * gw:(g + 1) * gw]
        y_off = jnp.dot(cg, s_old.astype(BF16), preferred_element_type=F32)
        decay = jnp.exp(alast_ref[0, 0:1, g * gw:(g + 1) * gw])
        ssq = jnp.zeros((CHUNK, LANES), F32)
        for j in range(HEADS_PER_GROUP // 2):
            c0 = g * gw + j * LANES
            xp = xbc_ref[:, c0:c0 + LANES].astype(F32)
            xbd = jnp.concatenate([jnp.where(lo, xp, 0.0), jnp.where(lo, 0.0, xp)],
                                  axis=0).astype(BF16)
            ms, ws, cols = [], [], []
            for h in (g * HEADS_PER_GROUP + 2 * j, g * HEADS_PER_GROUP + 2 * j + 1):
                colb = jnp.broadcast_to(acol_ref[:, h:h + 1], (CHUNK, CHUNK))
                seg = colb - rowp_ref[0, h:h + 1, :]
                lmat = jnp.exp2(jnp.where(causal, seg, -jnp.inf))
                ms.append((cb * lmat).astype(BF16))
                ws.append((bgt * wrow_ref[0, h:h + 1, :]).astype(BF16))
                cols.append(colb)
            y_diag = jnp.dot(jnp.concatenate(ms, axis=1), xbd, preferred_element_type=F32)
            s_new = jnp.dot(jnp.concatenate(ws, axis=1), xbd, preferred_element_type=F32)
            e = jnp.exp2(jnp.where(lo, cols[0], cols[1]))
            jl = slice(j * LANES, (j + 1) * LANES)
            y = y_diag + y_off[:, jl] * e + dskip_ref[:, c0:c0 + LANES] * xp
            st_ref[:, c0:c0 + LANES] = s_old[:, jl] * decay[:, jl] + s_new
            v = y * zs_ref[:, c0:c0 + LANES].astype(F32)
            ssq = ssq + v * v
            vt_ref[:, jl] = v
        sq_hi = ssq.astype(BF16)
        sq_lo = (ssq - sq_hi.astype(F32)).astype(BF16)
        tot = jnp.dot(jnp.concatenate([sq_hi, sq_lo], axis=1), ones, preferred_element_type=F32)
        scale = lax.rsqrt(tot * (1.0 / gw) + EPS)
        for j in range(HEADS_PER_GROUP // 2):
            c0 = g * gw + j * LANES
            jl = slice(j * LANES, (j + 1) * LANES)
            v_ref[:, c0:c0 + LANES] = (
                vt_ref[:, jl] * scale * ng_ref[:, c0:c0 + LANES]).astype(v_ref.dtype)


def _ssd(xbc, acol, rowp, wrow, alast_exp, dskip_exp, zg, norm_g, batch, seq):
    m = xbc.shape[0]
    nc = seq // CHUNK
    tab = pl.BlockSpec((1, HEAD_PAD, CHUNK), lambda b, c: (b * nc + c, 0, 0))
    return pl.pallas_call(
        _ssd_kernel,
        grid=(batch, nc),
        in_specs=[
            pl.BlockSpec((CHUNK, CONV_B_DIM), lambda b, c: (b * nc + c, 0)),
            pl.BlockSpec((CHUNK, HEAD_PAD), lambda b, c: (b * nc + c, 0)),
            tab,
            tab,
            pl.BlockSpec((1, SUBLANES, D_INNER), lambda b, c: (b * nc + c, 0, 0)),
            pl.BlockSpec((1, D_INNER), lambda b, c: (0, 0)),
            pl.BlockSpec((CHUNK, D_INNER), lambda b, c: (b * nc + c, 0)),
            pl.BlockSpec((1, D_INNER), lambda b, c: (0, 0)),
        ],
        out_specs=pl.BlockSpec((CHUNK, D_INNER), lambda b, c: (b * nc + c, 0)),
        out_shape=jax.ShapeDtypeStruct((m, D_INNER), BF16),
        scratch_shapes=[
            pltpu.VMEM((D_STATE, D_INNER), F32),
            pltpu.VMEM((CHUNK, HEADS_PER_GROUP * HEAD_DIM), F32),
        ],
        compiler_params=_params(("parallel", "arbitrary")),
        name="ssd",
    )(xbc, acol, rowp, wrow, alast_exp, dskip_exp, zg, norm_g)


def _merge_kernel(v_ref, u_ref, ga_ref, gb_ref, wss_ref, wco_ref, o_ref):
    y_b = jnp.dot(v_ref[...], wss_ref[...], preferred_element_type=F32)
    y_a = jnp.dot(u_ref[...], wco_ref[...], preferred_element_type=F32)
    o_ref[...] = (ga_ref[...].astype(F32) * y_a + gb_ref[...].astype(F32) * y_b).astype(o_ref.dtype)


def _merge(v, zg, u, w_ssm_out, w_conv_out, tm=1024, tn=512):
    m = v.shape[0]
    n = D_MODEL
    return pl.pallas_call(
        _merge_kernel,
        grid=(m // tm, n // tn),
        in_specs=[
            pl.BlockSpec((tm, D_INNER), lambda i, j: (i, 0)),
            pl.BlockSpec((tm, D_MODEL), lambda i, j: (i, 0)),
            pl.BlockSpec((tm, tn), lambda i, j: (i, D_INNER // tn + j)),
            pl.BlockSpec((tm, tn), lambda i, j: (i, (D_INNER + n) // tn + j)),
            pl.BlockSpec((D_INNER, tn), lambda i, j: (0, j)),
            pl.BlockSpec((D_MODEL, tn), lambda i, j: (0, j)),
        ],
        out_specs=pl.BlockSpec((tm, tn), lambda i, j: (i, j)),
        out_shape=jax.ShapeDtypeStruct((m, n), BF16),
        compiler_params=_params(("parallel", "arbitrary")),
        name="merge",
    )(v, u, zg, zg, w_ssm_out, w_conv_out)


def _wo_kernel(a_ref, w_ref, x_ref, g_ref, xo_ref, h_ref, wbf_ref):
    @pl.when(pl.program_id(0) == 0)
    def _():
        _cast_rows(w_ref, wbf_ref)

    xo_ref[...] = x_ref[...] + jnp.dot(a_ref[...], wbf_ref[...], preferred_element_type=F32)
    _rmsnorm_rows(xo_ref, g_ref, h_ref, xo_ref.shape[0])


def _wo(a, w_o, layer, x, g, tm=256):
    m, k = a.shape
    n = D_MODEL
    return pl.pallas_call(
        _wo_kernel,
        grid=(m // tm,),
        in_specs=[
            pl.BlockSpec((tm, k), lambda i: (i, 0)),
            pl.BlockSpec((None, k, n), lambda i: (layer, 0, 0), pipeline_mode=pl.Buffered(1)),
            pl.BlockSpec((tm, n), lambda i: (i, 0)),
            pl.BlockSpec((1, n), lambda i: (0, 0)),
        ],
        out_specs=[
            pl.BlockSpec((tm, n), lambda i: (i, 0)),
            pl.BlockSpec((tm, n), lambda i: (i, 0)),
        ],
        out_shape=[jax.ShapeDtypeStruct((m, n), F32), jax.ShapeDtypeStruct((m, n), BF16)],
        scratch_shapes=[pltpu.VMEM((k, n), BF16)],
        compiler_params=_params(("arbitrary",)),
        name="wo",
    )(a, w_o, x, g)


def _ffn_up_kernel(h_ref, wg_ref, wu_ref, o_ref, wbf_ref):
    @pl.when(pl.program_id(1) == 0)
    def _():
        _cast_rows(wg_ref, wbf_ref, 0)
        _cast_rows(wu_ref, wbf_ref, wg_ref.shape[1])

    tn = o_ref.shape[1]
    r = jnp.dot(h_ref[...], wbf_ref[...], preferred_element_type=F32)
    o_ref[...] = (_silu(r[:, :tn]) * r[:, tn:]).astype(o_ref.dtype)


def _ffn_up(h, w_gate, w_up, layer, tm=1024, tn=512):
    m, d = h.shape
    n = w_gate.shape[2]
    wspec = pl.BlockSpec((None, d, tn), lambda j, i: (layer, 0, j))
    return pl.pallas_call(
        _ffn_up_kernel,
        grid=(n // tn, m // tm),
        in_specs=[pl.BlockSpec((tm, d), lambda j, i: (i, 0)), wspec, wspec],
        out_specs=pl.BlockSpec((tm, tn), lambda j, i: (i, j)),
        out_shape=jax.ShapeDtypeStruct((m, n), BF16),
        scratch_shapes=[pltpu.VMEM((d, 2 * tn), BF16)],
        compiler_params=_params(("parallel", "arbitrary")),
        name="ffn_up",
    )(h, w_gate, w_up)


def _ffn_down_kernel(a_ref, w_ref, x_ref, g_ref, *refs, keep_residual):
    acc_ref, h_ref = refs if keep_residual else refs[::-1]
    k = pl.program_id(1)

    @pl.when(k == 0)
    def _():
        acc_ref[...] = x_ref[...]

    acc_ref[...] += jnp.dot(a_ref[...], w_ref[...], preferred_element_type=F32)

    @pl.when(k == pl.num_programs(1) - 1)
    def _():
        _rmsnorm_rows(acc_ref, g_ref, h_ref, acc_ref.shape[0])


def _ffn_down(a, w, x, g, keep_residual, tm=1024, tk=1408):
    m, kdim = a.shape
    n = w.shape[1]
    blk = pl.BlockSpec((tm, n), lambda i, k: (i, 0))
    if keep_residual:
        out_specs = [blk, blk]
        out_shape = [jax.ShapeDtypeStruct((m, n), F32), jax.ShapeDtypeStruct((m, n), BF16)]
        scratch = []
    else:
        out_specs = [blk]
        out_shape = [jax.ShapeDtypeStruct((m, n), F32)]
        scratch = [pltpu.VMEM((tm, n), F32)]
    return pl.pallas_call(
        functools.partial(_ffn_down_kernel, keep_residual=keep_residual),
        grid=(m // tm, kdim // tk),
        in_specs=[
            pl.BlockSpec((tm, tk), lambda i, k: (i, k)),
            pl.BlockSpec((tk, n), lambda i, k: (k, 0)),
            pl.BlockSpec((tm, n), lambda i, k: (i, 0), pipeline_mode=pl.Buffered(1)),
            pl.BlockSpec((1, n), lambda i, k: (0, 0)),
        ],
        out_specs=out_specs,
        out_shape=out_shape,
        scratch_shapes=scratch,
        compiler_params=_params(("parallel", "arbitrary")),
        name="ffn_down",
    )(a, w, x, g)


def _pad_heads(v):
    return jnp.pad(v.astype(F32), (0, HEAD_PAD - N_HEADS)).reshape(1, HEAD_PAD)


def _row(v):
    return v.astype(F32).reshape(1, -1)


def kernel(x, norm_mix_g, w_in, conv_a_w, conv_a_b, ln_a_g, ln_a_b, w_conv_out, conv_b_w, conv_b_b, dt_bias, a_log, d_skip, ssm_norm_g, w_ssm_out, w_o, norm_ffn_g, w_gate, w_up, w_down, final_g):
    batch, seq, d = x.shape
    depth = w_in.shape[0]
    xf = x.reshape(batch * seq, d)
    h = _rmsnorm(xf, _row(norm_mix_g[0]))
    w_in_t = jnp.swapaxes(w_in, 1, 2)
    for l in range(depth):
        u = _proj_glu(h, w_in_t, l)
        cwb = jnp.pad(conv_b_w[l].astype(F32), ((0, SUBLANES - K_CONV_B), (0, 0)))
        xbc = _proj_xbc(h, w_in_t, l, cwb, _row(conv_b_b[l]), seq)
        cw8 = jnp.repeat(conv_a_w[l].astype(F32), SUBLANES, axis=0)
        zg, ua = _proj_zg(h, w_in_t, l, u, cw8, _row(conv_a_b[l]), _row(ln_a_g[l]),
                          _row(ln_a_b[l]), seq)
        w_dt = jnp.pad(w_in_t[l, COL_DT:COL_DT + N_HEADS, :],
                       ((0, HEAD_PAD - N_HEADS), (0, 0)))
        acol, rowp, wrow, alast = _dt_tables(h, w_dt, _pad_heads(dt_bias[l]), _pad_heads(a_log[l]))
        alast_exp = jnp.repeat(alast[:, :, :N_HEADS], HEAD_DIM, axis=2)
        dskip_exp = jnp.repeat(d_skip[l].astype(F32), HEAD_DIM).reshape(1, D_INNER)
        v = _ssd(xbc, acol, rowp, wrow, alast_exp, dskip_exp, zg, _row(ssm_norm_g[l]), batch, seq)
        merged = _merge(v, zg, ua, w_ssm_out[l].astype(BF16), w_conv_out[l].astype(BF16))
        xf, h = _wo(merged, w_o, l, xf, _row(norm_ffn_g[l]))
        act = _ffn_up(h, w_gate, w_up, l)
        w_dn = w_down[l].astype(BF16)
        if l < depth - 1:
            xf, h = _ffn_down(act, w_dn, xf, _row(norm_mix_g[l + 1]), keep_residual=True)
        else:
            (h,) = _ffn_down(act, w_dn, xf, _row(final_g), keep_residual=False)
    return h.reshape(batch, seq, d)
```

```python
import functools

import jax
import jax.numpy as jnp
from jax import lax
from jax.experimental import pallas as pl
from jax.experimental.pallas import tpu as pltpu

F32 = jnp.float32
BF16 = jnp.bfloat16

D_MODEL = 2048
K_CONV_A = 31
D_INNER = 2 * D_MODEL
HEAD_DIM = 64
N_HEADS = D_INNER // HEAD_DIM
N_GROUPS = 8
HEADS_PER_GROUP = N_HEADS // N_GROUPS
D_STATE = 128
K_CONV_B = 4
BC_DIM = N_GROUPS * D_STATE
CONV_B_DIM = D_INNER + 2 * BC_DIM
CHUNK = 128
EPS = 1e-6
LOG2_E = 1.4426950408889634

COL_GLU = 0
COL_Z = 2 * D_MODEL
COL_XBC = COL_Z + D_INNER
COL_DT = COL_XBC + CONV_B_DIM
COL_GATES = COL_DT + N_HEADS
GATE_SKEW = N_HEADS

LANES = 128
SUBLANES = 8
HEAD_PAD = LANES
VMEM_LIMIT = 56 * 1024 * 1024


def _params(sem):
    return pltpu.CompilerParams(dimension_semantics=sem, vmem_limit_bytes=VMEM_LIMIT)


def _rows_loop(n_rows, blk, fn, unroll=1):
    def body(i, carry):
        fn(pl.multiple_of(i * blk, blk))
        return carry
    lax.fori_loop(0, n_rows // blk, body, 0, unroll=unroll)


def _cast_rows(src_ref, dst_ref, col0=0):
    w = src_ref.shape[1]
    def blk(r0):
        dst_ref[pl.ds(r0, 32), col0:col0 + w] = src_ref[pl.ds(r0, 32), :].astype(BF16)
    _rows_loop(src_ref.shape[0], 32, blk)


T_BLK = 256


def _cast_transposed(get_block, n_rows, dst_ref, col0=0):
    for n0 in range(0, n_rows, T_BLK):
        def body(kb, carry, n0=n0):
            k0 = pl.multiple_of(kb * T_BLK, T_BLK)
            dst_ref[pl.ds(k0, T_BLK), col0 + n0:col0 + n0 + T_BLK] = (
                get_block(n0, k0).T.astype(BF16))
            return carry
        lax.fori_loop(0, dst_ref.shape[0] // T_BLK, body, 0)


def _block_of(ref):
    return lambda n0, k0: ref[n0:n0 + T_BLK, pl.ds(k0, T_BLK)]


def _rmsnorm_rows(x_ref, g_ref, h_ref, n_rows):
    def blk(r0):
        x = x_ref[pl.ds(r0, 16), :]
        ms = jnp.mean(x * x, axis=-1, keepdims=True)
        h_ref[pl.ds(r0, 16), :] = (x * lax.rsqrt(ms + EPS) * g_ref[...]).astype(h_ref.dtype)
    _rows_loop(n_rows, 16, blk, unroll=8)


def _silu(v):
    return v * jax.nn.sigmoid(v)


def _rmsnorm_kernel(x_ref, g_ref, h_ref):
    _rmsnorm_rows(x_ref, g_ref, h_ref, x_ref.shape[0])


def _rmsnorm(x, g, tm=512):
    m, d = x.shape
    return pl.pallas_call(
        _rmsnorm_kernel,
        grid=(m // tm,),
        in_specs=[pl.BlockSpec((tm, d), lambda i: (i, 0)), pl.BlockSpec((1, d), lambda i: (0, 0))],
        out_specs=pl.BlockSpec((tm, d), lambda i: (i, 0)),
        out_shape=jax.ShapeDtypeStruct((m, d), BF16),
        compiler_params=_params(("parallel",)),
        name="rmsnorm",
    )(x, g)


def _proj_glu_kernel(h_ref, wa_ref, wb_ref, o_ref, wbf_ref):
    tn = o_ref.shape[1]

    @pl.when(pl.program_id(1) == 0)
    def _():
        _cast_transposed(_block_of(wa_ref), tn, wbf_ref, 0)
        _cast_transposed(_block_of(wb_ref), tn, wbf_ref, tn)

    r = jnp.dot(h_ref[...], wbf_ref[...], preferred_element_type=F32)
    o_ref[...] = (r[:, :tn] * jax.nn.sigmoid(r[:, tn:])).astype(o_ref.dtype)


def _proj_glu(h, w_in_t, layer, tm=1024, tn=512):
    m, d = h.shape
    nb = D_MODEL // tn
    return pl.pallas_call(
        _proj_glu_kernel,
        grid=(nb, m // tm),
        in_specs=[
            pl.BlockSpec((tm, d), lambda j, i: (i, 0)),
            pl.BlockSpec((None, tn, d), lambda j, i: (layer, COL_GLU // tn + j, 0)),
            pl.BlockSpec((None, tn, d), lambda j, i: (layer, COL_GLU // tn + nb + j, 0)),
        ],
        out_specs=pl.BlockSpec((tm, tn), lambda j, i: (i, j)),
        out_shape=jax.ShapeDtypeStruct((m, D_MODEL), BF16),
        scratch_shapes=[pltpu.VMEM((d, 2 * tn), BF16)],
        compiler_params=_params(("parallel", "arbitrary")),
        name="proj_glu",
    )(h, w_in_t, w_in_t)


HIST_A = 32
TT_A = 128
N_CHUNK = 256
ROWS_CHUNK = 32
WIN_A = 64
LANE_BLK_A = 256


def _conv_a_rows(s_ref, cw8_ref, cb_ref, cv_ref, r0):
    base = HIST_A - (K_CONV_A - 1)
    nq = ROWS_CHUNK // SUBLANES
    for c0 in range(0, D_MODEL, LANE_BLK_A):
        win = s_ref[pl.ds(r0, WIN_A), c0:c0 + LANE_BLK_A]
        bias = jnp.broadcast_to(cb_ref[:, c0:c0 + LANE_BLK_A], (SUBLANES, LANE_BLK_A))
        acc = [bias] * nq
        for b in range(SUBLANES):
            wb = win if b == 0 else pltpu.roll(win, WIN_A - b, 0)
            n_valid = WIN_A if b == 0 else WIN_A - SUBLANES
            for a in range(WIN_A // SUBLANES):
                k = a * SUBLANES + b - base
                if k < 0 or k >= K_CONV_A or (a + nq) * SUBLANES > n_valid:
                    continue
                w = cw8_ref[k * SUBLANES:(k + 1) * SUBLANES, c0:c0 + LANE_BLK_A]
                for q in range(nq):
                    acc[q] = acc[q] + wb[(a + q) * SUBLANES:(a + q + 1) * SUBLANES, :] * w
        for q in range(nq):
            cv_ref[q * SUBLANES:(q + 1) * SUBLANES, c0:c0 + LANE_BLK_A] = acc[q]


def _ln_silu_rows(cv_ref, g_ref, b_ref, o_ref, r0):
    for q in range(ROWS_CHUNK // 16):
        v = cv_ref[q * 16:(q + 1) * 16, :]
        mu = jnp.mean(v, axis=-1, keepdims=True)
        vc = v - mu
        var = jnp.mean(vc * vc, axis=-1, keepdims=True)
        y = vc * lax.rsqrt(var + EPS) * g_ref[...] + b_ref[...]
        o_ref[pl.ds(r0 + q * 16, 16), :] = _silu(y).astype(o_ref.dtype)


def _proj_zg_kernel(h_ref, wz_ref, wga_ref, wgb_ref, u_ref, cw8_ref, cb_ref, lng_ref, lnb_ref,
                    o_ref, ua_ref, wbf_ref, s_ref, cv_ref, *, n_z, tiles_per_seq):
    j = pl.program_id(0)
    i = pl.program_id(1)
    d, tn = wbf_ref.shape

    @pl.when((i == 0) & (j < n_z))
    def _():
        _cast_transposed(_block_of(wz_ref), tn, wbf_ref)

    @pl.when((i == 0) & (j >= n_z))
    def _():
        def gate_block(n0, k0):
            lo, hi = n0 + GATE_SKEW, n0 + GATE_SKEW + T_BLK
            if hi <= tn:
                return wga_ref[lo:hi, pl.ds(k0, T_BLK)]
            return jnp.concatenate([wga_ref[lo:tn, pl.ds(k0, T_BLK)],
                                    wgb_ref[0:hi - tn, pl.ds(k0, T_BLK)]], axis=0)
        _cast_transposed(gate_block, tn, wbf_ref)

    t = j * pl.num_programs(1) + i

    @pl.when(t % tiles_per_seq == 0)
    def _():
        s_ref[0:HIST_A, :] = jnp.zeros((HIST_A, s_ref.shape[1]), F32)

    @pl.when(t % tiles_per_seq != 0)
    def _():
        s_ref[0:HIST_A, :] = s_ref[TT_A:TT_A + HIST_A, :]

    is_z = j < n_z

    def chunk(c, carry):
        r0 = pl.multiple_of(c * ROWS_CHUNK, ROWS_CHUNK)
        s_ref[pl.ds(HIST_A + r0, ROWS_CHUNK), :] = u_ref[pl.ds(r0, ROWS_CHUNK), :].astype(F32)
        _conv_a_rows(s_ref, cw8_ref, cb_ref, cv_ref, r0)
        _ln_silu_rows(cv_ref, lng_ref, lnb_ref, ua_ref, r0)
        n0 = pl.multiple_of(c * N_CHUNK, N_CHUNK)
        r = jnp.dot(h_ref[...], wbf_ref[:, pl.ds(n0, N_CHUNK)], preferred_element_type=F32)
        sg = jax.nn.sigmoid(r)
        o_ref[:, pl.ds(n0, N_CHUNK)] = jnp.where(is_z, r * sg, sg).astype(o_ref.dtype)
        return carry
    lax.fori_loop(0, tn // N_CHUNK, chunk, 0)


def _proj_zg(h, w_in_t, layer, u, cw8, conv_b, ln_g, ln_b, seq, tm=1024, tn=1024):
    m, d = h.shape
    n_z = D_INNER // tn
    n_g = 2 * D_MODEL // tn
    n_i = m // tm
    assert (n_z + n_g) * n_i * TT_A == m and tn // N_CHUNK * ROWS_CHUNK == TT_A
    gate_tile0 = (COL_GATES - GATE_SKEW) // tn
    one = pl.Buffered(1)
    row = pl.BlockSpec((1, D_MODEL), lambda j, i: (0, 0))

    def gate_tile(j):
        return gate_tile0 + jnp.maximum(j - n_z, 0)

    return pl.pallas_call(
        functools.partial(_proj_zg_kernel, n_z=n_z, tiles_per_seq=seq // TT_A),
        grid=(n_z + n_g, n_i),
        in_specs=[
            pl.BlockSpec((tm, d), lambda j, i: (i, 0)),
            pl.BlockSpec((None, tn, d),
                         lambda j, i: (layer, COL_Z // tn + jnp.minimum(j, n_z - 1), 0),
                         pipeline_mode=one),
            pl.BlockSpec((None, tn, d), lambda j, i: (layer, gate_tile(j), 0), pipeline_mode=one),
            pl.BlockSpec((None, GATE_SKEW, d),
                         lambda j, i: (layer, (gate_tile(j) + 1) * (tn // GATE_SKEW), 0),
                         pipeline_mode=one),
            pl.BlockSpec((TT_A, D_MODEL), lambda j, i: (j * n_i + i, 0)),
            pl.BlockSpec((K_CONV_A * SUBLANES, D_MODEL), lambda j, i: (0, 0)),
            row, row, row,
        ],
        out_specs=[
            pl.BlockSpec((tm, tn), lambda j, i: (i, j)),
            pl.BlockSpec((TT_A, D_MODEL), lambda j, i: (j * n_i + i, 0)),
        ],
        out_shape=[
            jax.ShapeDtypeStruct((m, (n_z + n_g) * tn), BF16),
            jax.ShapeDtypeStruct((m, D_MODEL), BF16),
        ],
        scratch_shapes=[
            pltpu.VMEM((d, tn), BF16),
            pltpu.VMEM((TT_A + HIST_A, D_MODEL), F32),
            pltpu.VMEM((ROWS_CHUNK, D_MODEL), F32),
        ],
        compiler_params=_params(("arbitrary", "arbitrary")),
        name="proj_zg",
    )(h, w_in_t, w_in_t, w_in_t, u, cw8, conv_b, ln_g, ln_b)


def _proj_xbc_kernel(h_ref, w_ref, cw_ref, cb_ref, o_ref, wbf_ref, hist_ref, *, tiles_per_seq):
    i = pl.program_id(1)

    @pl.when(i == 0)
    def _():
        _cast_transposed(_block_of(w_ref), w_ref.shape[0], wbf_ref)

    @pl.when(i % tiles_per_seq == 0)
    def _():
        hist_ref[...] = jnp.zeros(hist_ref.shape, F32)

    tm = o_ref.shape[0]
    r = jnp.dot(h_ref[...], wbf_ref[...], preferred_element_type=F32)
    ext = jnp.concatenate([hist_ref[...], r], axis=0)
    w0, w1, w2, w3 = (cw_ref[k:k + 1, :] for k in range(K_CONV_B))
    d1 = pltpu.roll(ext, 1, 0)
    d2 = pltpu.roll(ext * w1 + d1 * w0, 2, 0)
    acc = cb_ref[...] + ext * w3 + d1 * w2 + d2
    o_ref[...] = _silu(acc[SUBLANES:, :]).astype(o_ref.dtype)
    hist_ref[...] = r[tm - SUBLANES:, :]


def _proj_xbc(h, w_in_t, layer, conv_w, conv_b, seq, tm=1024, tn=1024):
    m, d = h.shape
    return pl.pallas_call(
        functools.partial(_proj_xbc_kernel, tiles_per_seq=seq // tm),
        grid=(CONV_B_DIM // tn, m // tm),
        in_specs=[
            pl.BlockSpec((tm, d), lambda j, i: (i, 0)),
            pl.BlockSpec((None, tn, d), lambda j, i: (layer, COL_XBC // tn + j, 0)),
            pl.BlockSpec((SUBLANES, tn), lambda j, i: (0, j)),
            pl.BlockSpec((1, tn), lambda j, i: (0, j)),
        ],
        out_specs=pl.BlockSpec((tm, tn), lambda j, i: (i, j)),
        out_shape=jax.ShapeDtypeStruct((m, CONV_B_DIM), BF16),
        scratch_shapes=[pltpu.VMEM((d, tn), BF16), pltpu.VMEM((SUBLANES, tn), F32)],
        compiler_params=_params(("parallel", "arbitrary")),
        name="proj_xbc",
    )(h, w_in_t, conv_w, conv_b)


def _split3(v):
    p1 = v.astype(BF16)
    r1 = v - p1.astype(F32)
    p2 = r1.astype(BF16)
    p3 = (r1 - p2.astype(F32)).astype(BF16)
    return p1, p2, p3


def _dt_kernel(h_ref, wdt_ref, bias_ref, alog_ref, acol_ref, rowp_ref, wrow_ref, alast_ref):
    x_all = lax.dot_general(h_ref[...], wdt_ref[...].astype(BF16), (((1,), (1,)), ((), ())),
                            preferred_element_type=F32) + bias_ref[...]
    a = -jnp.exp(alog_ref[...])
    li = lax.broadcasted_iota(jnp.int32, (CHUNK, CHUNK), 0)
    si = lax.broadcasted_iota(jnp.int32, (CHUNK, CHUNK), 1)
    tril = jnp.where(li >= si, 1.0, 0.0).astype(BF16)
    for q in range(h_ref.shape[0] // CHUNK):
        x = x_all[q * CHUNK:(q + 1) * CHUNK, :]
        dt = jnp.maximum(x, 0.0) + jnp.log1p(jnp.exp(-jnp.abs(x)))
        adt = dt * a
        acs = None
        for p in _split3(adt):
            term = jnp.dot(tril, p, preferred_element_type=F32)
            acs = term if acs is None else acs + term
        acol_ref[q * CHUNK:(q + 1) * CHUNK, :] = acs * LOG2_E
        alast = acs[CHUNK - 1:CHUNK, :]
        alast_ref[q] = jnp.broadcast_to(alast, (SUBLANES, HEAD_PAD))
        rowp_ref[q] = ((acs - jnp.log(dt)) * LOG2_E).T
        wrow_ref[q] = (jnp.exp(alast - acs) * dt).T


def _dt_tables(h, w_dt, dt_bias, a_log, chunks_per_step=4):
    m, d = h.shape
    nchunks = m // CHUNK
    cps = chunks_per_step
    row = pl.BlockSpec((1, HEAD_PAD), lambda i: (0, 0))
    tab = pl.BlockSpec((cps, HEAD_PAD, CHUNK), lambda i: (i, 0, 0))
    return pl.pallas_call(
        _dt_kernel,
        grid=(nchunks // cps,),
        in_specs=[
            pl.BlockSpec((cps * CHUNK, d), lambda i: (i, 0)),
            pl.BlockSpec((HEAD_PAD, d), lambda i: (0, 0)),
            row,
            row,
        ],
        out_specs=[
            pl.BlockSpec((cps * CHUNK, HEAD_PAD), lambda i: (i, 0)),
            tab,
            tab,
            pl.BlockSpec((cps, SUBLANES, HEAD_PAD), lambda i: (i, 0, 0)),
        ],
        out_shape=[
            jax.ShapeDtypeStruct((m, HEAD_PAD), F32),
            jax.ShapeDtypeStruct((nchunks, HEAD_PAD, CHUNK), F32),
            jax.ShapeDtypeStruct((nchunks, HEAD_PAD, CHUNK), F32),
            jax.ShapeDtypeStruct((nchunks, SUBLANES, HEAD_PAD), F32),
        ],
        compiler_params=_params(("parallel",)),
        name="dt_tables",
    )(h, w_dt, dt_bias, a_log)


def _ssd_kernel(xbc_ref, acol_ref, rowp_ref, wrow_ref, alast_ref, dskip_ref, zs_ref, ng_ref,
                v_ref, st_ref, vt_ref):
    @pl.when(pl.program_id(1) == 0)
    def _():
        st_ref[...] = jnp.zeros(st_ref.shape, F32)

    li = lax.broadcasted_iota(jnp.int32, (CHUNK, CHUNK), 0)
    si = lax.broadcasted_iota(jnp.int32, (CHUNK, CHUNK), 1)
    causal = li >= si
    lo = si < HEAD_DIM
    gw = HEADS_PER_GROUP * HEAD_DIM
    ones = jnp.ones((2 * LANES, LANES), BF16)

    for g in range(N_GROUPS):
        bg = xbc_ref[:, D_INNER + g * D_STATE:D_INNER + (g + 1) * D_STATE]
        cg = xbc_ref[:, D_INNER + BC_DIM + g * D_STATE:D_INNER + BC_DIM + (g + 1) * D_STATE]
        cb = lax.dot_general(cg, bg, (((1,), (1,)), ((), ())), preferred_element_type=F32)
        bgt = bg.astype(F32).T
        ssq = jnp.zeros((CHUNK, LANES), F32)
        for j in range(HEADS_PER_GROUP // 2):
            c0 = g * gw + j * LANES
            s_old = st_ref[:, c0:c0 + LANES]
            y_off = jnp.dot(cg, s_old.astype(BF16), preferred_element_type=F32)
            decay = jnp.exp(alast_ref[0, 0:1, c0:c0 + LANES])
            xp = xbc_ref[:, c0:c0 + LANES].astype(F32)
            xbd = jnp.concatenate([jnp.where(lo, xp, 0.0), jnp.where(lo, 0.0, xp)],
                                  axis=0).astype(BF16)
            ms, ws, cols = [], [], []
            for h in (g * HEADS_PER_GROUP + 2 * j, g * HEADS_PER_GROUP + 2 * j + 1):
                colb = jnp.broadcast_to(acol_ref[:, h:h + 1], (CHUNK, CHUNK))
                seg = colb - rowp_ref[0, h:h + 1, :]
                lmat = jnp.exp2(jnp.where(causal, seg, -jnp.inf))
                ms.append((cb * lmat).astype(BF16))
                ws.append((bgt * wrow_ref[0, h:h + 1, :]).astype(BF16))
                cols.append(colb)
            y_diag = jnp.dot(jnp.concatenate(ms, axis=1), xbd, preferred_element_type=F32)
            s_new = jnp.dot(jnp.concatenate(ws, axis=1), xbd, preferred_element_type=F32)
            e = jnp.exp2(jnp.where(lo, cols[0], cols[1]))
            jl = slice(j * LANES, (j + 1) * LANES)
            y = y_diag + y_off * e + dskip_ref[:, c0:c0 + LANES] * xp
            st_ref[:, c0:c0 + LANES] = s_old * decay + s_new
            v = y * zs_ref[:, c0:c0 + LANES].astype(F32)
            ssq = ssq + v * v
            vt_ref[:, jl] = v
        sq_hi = ssq.astype(BF16)
        sq_lo = (ssq - sq_hi.astype(F32)).astype(BF16)
        tot = jnp.dot(jnp.concatenate([sq_hi, sq_lo], axis=1), ones, preferred_element_type=F32)
        scale = lax.rsqrt(tot * (1.0 / gw) + EPS)
        for j in range(HEADS_PER_GROUP // 2):
            c0 = g * gw + j * LANES
            jl = slice(j * LANES, (j + 1) * LANES)
            v_ref[:, c0:c0 + LANES] = (
                vt_ref[:, jl] * scale * ng_ref[:, c0:c0 + LANES]).astype(v_ref.dtype)


def _ssd(xbc, acol, rowp, wrow, alast_exp, dskip_exp, zg, norm_g, batch, seq):
    m = xbc.shape[0]
    nc = seq // CHUNK
    tab = pl.BlockSpec((1, HEAD_PAD, CHUNK), lambda b, c: (b * nc + c, 0, 0))
    return pl.pallas_call(
        _ssd_kernel,
        grid=(batch, nc),
        in_specs=[
            pl.BlockSpec((CHUNK, CONV_B_DIM), lambda b, c: (b * nc + c, 0)),
            pl.BlockSpec((CHUNK, HEAD_PAD), lambda b, c: (b * nc + c, 0)),
            tab,
            tab,
            pl.BlockSpec((1, SUBLANES, D_INNER), lambda b, c: (b * nc + c, 0, 0)),
            pl.BlockSpec((1, D_INNER), lambda b, c: (0, 0)),
            pl.BlockSpec((CHUNK, D_INNER), lambda b, c: (b * nc + c, 0)),
            pl.BlockSpec((1, D_INNER), lambda b, c: (0, 0)),
        ],
        out_specs=pl.BlockSpec((CHUNK, D_INNER), lambda b, c: (b * nc + c, 0)),
        out_shape=jax.ShapeDtypeStruct((m, D_INNER), BF16),
        scratch_shapes=[
            pltpu.VMEM((D_STATE, D_INNER), F32),
            pltpu.VMEM((CHUNK, HEADS_PER_GROUP * HEAD_DIM), F32),
        ],
        compiler_params=_params(("parallel", "arbitrary")),
        name="ssd",
    )(xbc, acol, rowp, wrow, alast_exp, dskip_exp, zg, norm_g)


def _merge_kernel(v_ref, u_ref, ga_ref, gb_ref, wss_ref, wco_ref, o_ref):
    y_b = jnp.dot(v_ref[...], wss_ref[...], preferred_element_type=F32)
    y_a = jnp.dot(u_ref[...], wco_ref[...], preferred_element_type=F32)
    o_ref[...] = (ga_ref[...].astype(F32) * y_a + gb_ref[...].astype(F32) * y_b).astype(o_ref.dtype)


def _merge(v, zg, u, w_ssm_out, w_conv_out, tm=1024, tn=512):
    m = v.shape[0]
    n = D_MODEL
    return pl.pallas_call(
        _merge_kernel,
        grid=(m // tm, n // tn),
        in_specs=[
            pl.BlockSpec((tm, D_INNER), lambda i, j: (i, 0)),
            pl.BlockSpec((tm, D_MODEL), lambda i, j: (i, 0)),
            pl.BlockSpec((tm, tn), lambda i, j: (i, D_INNER // tn + j)),
            pl.BlockSpec((tm, tn), lambda i, j: (i, (D_INNER + n) // tn + j)),
            pl.BlockSpec((D_INNER, tn), lambda i, j: (0, j)),
            pl.BlockSpec((D_MODEL, tn), lambda i, j: (0, j)),
        ],
        out_specs=pl.BlockSpec((tm, tn), lambda i, j: (i, j)),
        out_shape=jax.ShapeDtypeStruct((m, n), BF16),
        compiler_params=_params(("parallel", "arbitrary")),
        name="merge",
    )(v, u, zg, zg, w_ssm_out, w_conv_out)


def _wo_kernel(a_ref, w_ref, x_ref, g_ref, xo_ref, h_ref, wbf_ref):
    @pl.when(pl.program_id(0) == 0)
    def _():
        _cast_rows(w_ref, wbf_ref)

    xo_ref[...] = x_ref[...] + jnp.dot(a_ref[...], wbf_ref[...], preferred_element_type=F32)
    _rmsnorm_rows(xo_ref, g_ref, h_ref, xo_ref.shape[0])


def _wo(a, w_o, layer, x, g, tm=256):
    m, k = a.shape
    n = D_MODEL
    return pl.pallas_call(
        _wo_kernel,
        grid=(m // tm,),
        in_specs=[
            pl.BlockSpec((tm, k), lambda i: (i, 0)),
            pl.BlockSpec((None, k, n), lambda i: (layer, 0, 0), pipeline_mode=pl.Buffered(1)),
            pl.BlockSpec((tm, n), lambda i: (i, 0)),
            pl.BlockSpec((1, n), lambda i: (0, 0)),
        ],
        out_specs=[
            pl.BlockSpec((tm, n), lambda i: (i, 0)),
            pl.BlockSpec((tm, n), lambda i: (i, 0)),
        ],
        out_shape=[jax.ShapeDtypeStruct((m, n), F32), jax.ShapeDtypeStruct((m, n), BF16)],
        scratch_shapes=[pltpu.VMEM((k, n), BF16)],
        compiler_params=_params(("arbitrary",)),
        name="wo",
    )(a, w_o, x, g)


def _ffn_up_kernel(h_ref, wg_ref, wu_ref, o_ref, wbf_ref):
    @pl.when(pl.program_id(1) == 0)
    def _():
        _cast_rows(wg_ref, wbf_ref, 0)
        _cast_rows(wu_ref, wbf_ref, wg_ref.shape[1])

    tn = o_ref.shape[1]
    r = jnp.dot(h_ref[...], wbf_ref[...], preferred_element_type=F32)
    o_ref[...] = (_silu(r[:, :tn]) * r[:, tn:]).astype(o_ref.dtype)


def _ffn_up(h, w_gate, w_up, layer, tm=1024, tn=512):
    m, d = h.shape
    n = w_gate.shape[2]
    wspec = pl.BlockSpec((None, d, tn), lambda j, i: (layer, 0, j))
    return pl.pallas_call(
        _ffn_up_kernel,
        grid=(n // tn, m // tm),
        in_specs=[pl.BlockSpec((tm, d), lambda j, i: (i, 0)), wspec, wspec],
        out_specs=pl.BlockSpec((tm, tn), lambda j, i: (i, j)),
        out_shape=jax.ShapeDtypeStruct((m, n), BF16),
        scratch_shapes=[pltpu.VMEM((d, 2 * tn), BF16)],
        compiler_params=_params(("parallel", "arbitrary")),
        name="ffn_up",
    )(h, w_gate, w_up)


def _ffn_down_kernel(a_ref, w_ref, x_ref, g_ref, *refs, keep_residual):
    acc_ref, h_ref = refs if keep_residual else refs[::-1]
    k = pl.program_id(1)

    @pl.when(k == 0)
    def _():
        acc_ref[...] = x_ref[...]

    acc_ref[...] += jnp.dot(a_ref[...], w_ref[...], preferred_element_type=F32)

    @pl.when(k == pl.num_programs(1) - 1)
    def _():
        _rmsnorm_rows(acc_ref, g_ref, h_ref, acc_ref.shape[0])


def _ffn_down(a, w, x, g, keep_residual, tm=1024, tk=512):
    m, kdim = a.shape
    n = w.shape[1]
    blk = pl.BlockSpec((tm, n), lambda i, k: (i, 0))
    if keep_residual:
        out_specs = [blk, blk]
        out_shape = [jax.ShapeDtypeStruct((m, n), F32), jax.ShapeDtypeStruct((m, n), BF16)]
        scratch = []
    else:
        out_specs = [blk]
        out_shape = [jax.ShapeDtypeStruct((m, n), F32)]
        scratch = [pltpu.VMEM((tm, n), F32)]
    return pl.pallas_call(
        functools.partial(_ffn_down_kernel, keep_residual=keep_residual),
        grid=(m // tm, kdim // tk),
        in_specs=[
            pl.BlockSpec((tm, tk), lambda i, k: (i, k)),
            pl.BlockSpec((tk, n), lambda i, k: (k, 0)),
            pl.BlockSpec((tm, n), lambda i, k: (i, 0)),
            pl.BlockSpec((1, n), lambda i, k: (0, 0)),
        ],
        out_specs=out_specs,
        out_shape=out_shape,
        scratch_shapes=scratch,
        compiler_params=_params(("parallel", "arbitrary")),
        name="ffn_down",
    )(a, w, x, g)


def _pad_heads(v):
    return jnp.pad(v.astype(F32), (0, HEAD_PAD - N_HEADS)).reshape(1, HEAD_PAD)


def _row(v):
    return v.astype(F32).reshape(1, -1)


def kernel(x, norm_mix_g, w_in, conv_a_w, conv_a_b, ln_a_g, ln_a_b, w_conv_out, conv_b_w, conv_b_b, dt_bias, a_log, d_skip, ssm_norm_g, w_ssm_out, w_o, norm_ffn_g, w_gate, w_up, w_down, final_g):
    batch, seq, d = x.shape
    depth = w_in.shape[0]
    xf = x.reshape(batch * seq, d)
    h = _rmsnorm(xf, _row(norm_mix_g[0]))
    w_in_t = jnp.swapaxes(w_in, 1, 2)
    for l in range(depth):
        u = _proj_glu(h, w_in_t, l)
        cwb = jnp.pad(conv_b_w[l].astype(F32), ((0, SUBLANES - K_CONV_B), (0, 0)))
        xbc = _proj_xbc(h, w_in_t, l, cwb, _row(conv_b_b[l]), seq)
        cw8 = jnp.repeat(conv_a_w[l].astype(F32), SUBLANES, axis=0)
        zg, ua = _proj_zg(h, w_in_t, l, u, cw8, _row(conv_a_b[l]), _row(ln_a_g[l]),
                          _row(ln_a_b[l]), seq)
        w_dt = jnp.pad(w_in_t[l, COL_DT:COL_DT + N_HEADS, :],
                       ((0, HEAD_PAD - N_HEADS), (0, 0)))
        acol, rowp, wrow, alast = _dt_tables(h, w_dt, _pad_heads(dt_bias[l]), _pad_heads(a_log[l]))
        alast_exp = jnp.repeat(alast[:, :, :N_HEADS], HEAD_DIM, axis=2)
        dskip_exp = jnp.repeat(d_skip[l].astype(F32), HEAD_DIM).reshape(1, D_INNER)
        v = _ssd(xbc, acol, rowp, wrow, alast_exp, dskip_exp, zg, _row(ssm_norm_g[l]), batch, seq)
        merged = _merge(v, zg, ua, w_ssm_out[l].astype(BF16), w_conv_out[l].astype(BF16))
        xf, h = _wo(merged, w_o, l, xf, _row(norm_ffn_g[l]))
        act = _ffn_up(h, w_gate, w_up, l)
        w_dn = w_down[l].astype(BF16)
        if l < depth - 1:
            xf, h = _ffn_down(act, w_dn, xf, _row(norm_mix_g[l + 1]), keep_residual=True)
        else:
            (h,) = _ffn_down(act, w_dn, xf, _row(final_g), keep_residual=False)
    return h.reshape(batch, seq, d)
```

```python
import functools

import jax
import jax.numpy as jnp
from jax import lax
from jax.experimental import pallas as pl
from jax.experimental.pallas import tpu as pltpu

F32 = jnp.float32
BF16 = jnp.bfloat16

D_MODEL = 2048
K_CONV_A = 31
D_INNER = 2 * D_MODEL
HEAD_DIM = 64
N_HEADS = D_INNER // HEAD_DIM
N_GROUPS = 8
HEADS_PER_GROUP = N_HEADS // N_GROUPS
D_STATE = 128
K_CONV_B = 4
BC_DIM = N_GROUPS * D_STATE
CONV_B_DIM = D_INNER + 2 * BC_DIM
CHUNK = 128
EPS = 1e-6
LOG2_E = 1.4426950408889634

COL_GLU = 0
COL_Z = 2 * D_MODEL
COL_XBC = COL_Z + D_INNER
COL_DT = COL_XBC + CONV_B_DIM
COL_GATES = COL_DT + N_HEADS
GATE_SKEW = N_HEADS

LANES = 128
SUBLANES = 8
HEAD_PAD = LANES
VMEM_LIMIT = 56 * 1024 * 1024


def _params(sem):
    return pltpu.CompilerParams(dimension_semantics=sem, vmem_limit_bytes=VMEM_LIMIT)


def _rows_loop(n_rows, blk, fn, unroll=1):
    def body(i, carry):
        fn(pl.multiple_of(i * blk, blk))
        return carry
    lax.fori_loop(0, n_rows // blk, body, 0, unroll=unroll)


def _cast_rows(src_ref, dst_ref, col0=0):
    w = src_ref.shape[1]
    def blk(r0):
        dst_ref[pl.ds(r0, 32), col0:col0 + w] = src_ref[pl.ds(r0, 32), :].astype(BF16)
    _rows_loop(src_ref.shape[0], 32, blk)


T_BLK = 256


def _cast_transposed(get_block, n_rows, dst_ref, col0=0):
    for n0 in range(0, n_rows, T_BLK):
        def body(kb, carry, n0=n0):
            k0 = pl.multiple_of(kb * T_BLK, T_BLK)
            dst_ref[pl.ds(k0, T_BLK), col0 + n0:col0 + n0 + T_BLK] = (
                get_block(n0, k0).T.astype(BF16))
            return carry
        lax.fori_loop(0, dst_ref.shape[0] // T_BLK, body, 0, unroll=4)


def _block_of(ref):
    return lambda n0, k0: ref[n0:n0 + T_BLK, pl.ds(k0, T_BLK)]


def _rmsnorm_rows(x_ref, g_ref, h_ref, n_rows):
    def blk(r0):
        x = x_ref[pl.ds(r0, 16), :]
        ms = jnp.mean(x * x, axis=-1, keepdims=True)
        h_ref[pl.ds(r0, 16), :] = (x * lax.rsqrt(ms + EPS) * g_ref[...]).astype(h_ref.dtype)
    _rows_loop(n_rows, 16, blk, unroll=8)


def _silu(v):
    return v * jax.nn.sigmoid(v)


def _rmsnorm_kernel(x_ref, g_ref, h_ref):
    _rmsnorm_rows(x_ref, g_ref, h_ref, x_ref.shape[0])


def _rmsnorm(x, g, tm=512):
    m, d = x.shape
    return pl.pallas_call(
        _rmsnorm_kernel,
        grid=(m // tm,),
        in_specs=[pl.BlockSpec((tm, d), lambda i: (i, 0)), pl.BlockSpec((1, d), lambda i: (0, 0))],
        out_specs=pl.BlockSpec((tm, d), lambda i: (i, 0)),
        out_shape=jax.ShapeDtypeStruct((m, d), BF16),
        compiler_params=_params(("parallel",)),
        name="rmsnorm",
    )(x, g)


def _proj_glu_kernel(h_ref, wa_ref, wb_ref, o_ref, wbf_ref):
    tn = o_ref.shape[1]

    @pl.when(pl.program_id(1) == 0)
    def _():
        _cast_transposed(_block_of(wa_ref), tn, wbf_ref, 0)
        _cast_transposed(_block_of(wb_ref), tn, wbf_ref, tn)

    r = jnp.dot(h_ref[...], wbf_ref[...], preferred_element_type=F32)
    o_ref[...] = (r[:, :tn] * jax.nn.sigmoid(r[:, tn:])).astype(o_ref.dtype)


def _proj_glu(h, w_in_t, layer, tm=1024, tn=512):
    m, d = h.shape
    nb = D_MODEL // tn
    return pl.pallas_call(
        _proj_glu_kernel,
        grid=(nb, m // tm),
        in_specs=[
            pl.BlockSpec((tm, d), lambda j, i: (i, 0)),
            pl.BlockSpec((None, tn, d), lambda j, i: (layer, COL_GLU // tn + j, 0)),
            pl.BlockSpec((None, tn, d), lambda j, i: (layer, COL_GLU // tn + nb + j, 0)),
        ],
        out_specs=pl.BlockSpec((tm, tn), lambda j, i: (i, j)),
        out_shape=jax.ShapeDtypeStruct((m, D_MODEL), BF16),
        scratch_shapes=[pltpu.VMEM((d, 2 * tn), BF16)],
        compiler_params=_params(("parallel", "arbitrary")),
        name="proj_glu",
    )(h, w_in_t, w_in_t)


HIST_A = 32
TT_A = 128
N_CHUNK = 256
ROWS_CHUNK = 32
WIN_A = 64
LANE_BLK_A = 256


def _conv_a_rows(s_ref, cw8_ref, cb_ref, cv_ref, r0):
    base = HIST_A - (K_CONV_A - 1)
    nq = ROWS_CHUNK // SUBLANES
    for c0 in range(0, D_MODEL, LANE_BLK_A):
        win = s_ref[pl.ds(r0, WIN_A), c0:c0 + LANE_BLK_A]
        bias = jnp.broadcast_to(cb_ref[:, c0:c0 + LANE_BLK_A], (SUBLANES, LANE_BLK_A))
        acc = [bias] * nq
        for b in range(SUBLANES):
            wb = win if b == 0 else pltpu.roll(win, WIN_A - b, 0)
            n_valid = WIN_A if b == 0 else WIN_A - SUBLANES
            for a in range(WIN_A // SUBLANES):
                k = a * SUBLANES + b - base
                if k < 0 or k >= K_CONV_A or (a + nq) * SUBLANES > n_valid:
                    continue
                w = cw8_ref[k * SUBLANES:(k + 1) * SUBLANES, c0:c0 + LANE_BLK_A]
                for q in range(nq):
                    acc[q] = acc[q] + wb[(a + q) * SUBLANES:(a + q + 1) * SUBLANES, :] * w
        for q in range(nq):
            cv_ref[q * SUBLANES:(q + 1) * SUBLANES, c0:c0 + LANE_BLK_A] = acc[q]


def _ln_silu_rows(cv_ref, g_ref, b_ref, o_ref, r0):
    for q in range(ROWS_CHUNK // 16):
        v = cv_ref[q * 16:(q + 1) * 16, :]
        mu = jnp.mean(v, axis=-1, keepdims=True)
        vc = v - mu
        var = jnp.mean(vc * vc, axis=-1, keepdims=True)
        y = vc * lax.rsqrt(var + EPS) * g_ref[...] + b_ref[...]
        o_ref[pl.ds(r0 + q * 16, 16), :] = _silu(y).astype(o_ref.dtype)


def _proj_zg_kernel(h_ref, wz_ref, wga_ref, wgb_ref, u_ref, cw8_ref, cb_ref, lng_ref, lnb_ref,
                    o_ref, ua_ref, wbf_ref, s_ref, cv_ref, *, n_z, tiles_per_seq):
    j = pl.program_id(0)
    i = pl.program_id(1)
    d, tn = wbf_ref.shape

    @pl.when((i == 0) & (j < n_z))
    def _():
        _cast_transposed(_block_of(wz_ref), tn, wbf_ref)

    @pl.when((i == 0) & (j >= n_z))
    def _():
        def gate_block(n0, k0):
            lo, hi = n0 + GATE_SKEW, n0 + GATE_SKEW + T_BLK
            if hi <= tn:
                return wga_ref[lo:hi, pl.ds(k0, T_BLK)]
            return jnp.concatenate([wga_ref[lo:tn, pl.ds(k0, T_BLK)],
                                    wgb_ref[0:hi - tn, pl.ds(k0, T_BLK)]], axis=0)
        _cast_transposed(gate_block, tn, wbf_ref)

    t = j * pl.num_programs(1) + i

    @pl.when(t % tiles_per_seq == 0)
    def _():
        s_ref[0:HIST_A, :] = jnp.zeros((HIST_A, s_ref.shape[1]), F32)

    @pl.when(t % tiles_per_seq != 0)
    def _():
        s_ref[0:HIST_A, :] = s_ref[TT_A:TT_A + HIST_A, :]

    is_z = j < n_z

    def chunk(c, carry):
        r0 = pl.multiple_of(c * ROWS_CHUNK, ROWS_CHUNK)
        s_ref[pl.ds(HIST_A + r0, ROWS_CHUNK), :] = u_ref[pl.ds(r0, ROWS_CHUNK), :].astype(F32)
        _conv_a_rows(s_ref, cw8_ref, cb_ref, cv_ref, r0)
        _ln_silu_rows(cv_ref, lng_ref, lnb_ref, ua_ref, r0)
        n0 = pl.multiple_of(c * N_CHUNK, N_CHUNK)
        r = jnp.dot(h_ref[...], wbf_ref[:, pl.ds(n0, N_CHUNK)], preferred_element_type=F32)
        sg = jax.nn.sigmoid(r)
        o_ref[:, pl.ds(n0, N_CHUNK)] = jnp.where(is_z, r * sg, sg).astype(o_ref.dtype)
        return carry
    lax.fori_loop(0, tn // N_CHUNK, chunk, 0)


def _proj_zg(h, w_in_t, layer, u, cw8, conv_b, ln_g, ln_b, seq, tm=1024, tn=1024):
    m, d = h.shape
    n_z = D_INNER // tn
    n_g = 2 * D_MODEL // tn
    n_i = m // tm
    assert (n_z + n_g) * n_i * TT_A == m and tn // N_CHUNK * ROWS_CHUNK == TT_A
    gate_tile0 = (COL_GATES - GATE_SKEW) // tn
    one = pl.Buffered(1)
    row = pl.BlockSpec((1, D_MODEL), lambda j, i: (0, 0))

    def gate_tile(j):
        return gate_tile0 + jnp.maximum(j - n_z, 0)

    return pl.pallas_call(
        functools.partial(_proj_zg_kernel, n_z=n_z, tiles_per_seq=seq // TT_A),
        grid=(n_z + n_g, n_i),
        in_specs=[
            pl.BlockSpec((tm, d), lambda j, i: (i, 0)),
            pl.BlockSpec((None, tn, d),
                         lambda j, i: (layer, COL_Z // tn + jnp.minimum(j, n_z - 1), 0),
                         pipeline_mode=one),
            pl.BlockSpec((None, tn, d), lambda j, i: (layer, gate_tile(j), 0), pipeline_mode=one),
            pl.BlockSpec((None, GATE_SKEW, d),
                         lambda j, i: (layer, (gate_tile(j) + 1) * (tn // GATE_SKEW), 0),
                         pipeline_mode=one),
            pl.BlockSpec((TT_A, D_MODEL), lambda j, i: (j * n_i + i, 0)),
            pl.BlockSpec((K_CONV_A * SUBLANES, D_MODEL), lambda j, i: (0, 0)),
            row, row, row,
        ],
        out_specs=[
            pl.BlockSpec((tm, tn), lambda j, i: (i, j)),
            pl.BlockSpec((TT_A, D_MODEL), lambda j, i: (j * n_i + i, 0)),
        ],
        out_shape=[
            jax.ShapeDtypeStruct((m, (n_z + n_g) * tn), BF16),
            jax.ShapeDtypeStruct((m, D_MODEL), BF16),
        ],
        scratch_shapes=[
            pltpu.VMEM((d, tn), BF16),
            pltpu.VMEM((TT_A + HIST_A, D_MODEL), F32),
            pltpu.VMEM((ROWS_CHUNK, D_MODEL), F32),
        ],
        compiler_params=_params(("arbitrary", "arbitrary")),
        name="proj_zg",
    )(h, w_in_t, w_in_t, w_in_t, u, cw8, conv_b, ln_g, ln_b)


def _proj_xbc_kernel(h_ref, w_ref, cw_ref, cb_ref, o_ref, wbf_ref, hist_ref, *, tiles_per_seq):
    i = pl.program_id(1)

    @pl.when(i == 0)
    def _():
        _cast_transposed(_block_of(w_ref), w_ref.shape[0], wbf_ref)

    @pl.when(i % tiles_per_seq == 0)
    def _():
        hist_ref[...] = jnp.zeros(hist_ref.shape, F32)

    tm = o_ref.shape[0]
    r = jnp.dot(h_ref[...], wbf_ref[...], preferred_element_type=F32)
    ext = jnp.concatenate([hist_ref[...], r], axis=0)
    w0, w1, w2, w3 = (cw_ref[k:k + 1, :] for k in range(K_CONV_B))
    d1 = pltpu.roll(ext, 1, 0)
    d2 = pltpu.roll(ext * w1 + d1 * w0, 2, 0)
    acc = cb_ref[...] + ext * w3 + d1 * w2 + d2
    o_ref[...] = _silu(acc[SUBLANES:, :]).astype(o_ref.dtype)
    hist_ref[...] = r[tm - SUBLANES:, :]


def _proj_xbc(h, w_in_t, layer, conv_w, conv_b, seq, tm=1024, tn=1024):
    m, d = h.shape
    return pl.pallas_call(
        functools.partial(_proj_xbc_kernel, tiles_per_seq=seq // tm),
        grid=(CONV_B_DIM // tn, m // tm),
        in_specs=[
            pl.BlockSpec((tm, d), lambda j, i: (i, 0)),
            pl.BlockSpec((None, tn, d), lambda j, i: (layer, COL_XBC // tn + j, 0)),
            pl.BlockSpec((SUBLANES, tn), lambda j, i: (0, j)),
            pl.BlockSpec((1, tn), lambda j, i: (0, j)),
        ],
        out_specs=pl.BlockSpec((tm, tn), lambda j, i: (i, j)),
        out_shape=jax.ShapeDtypeStruct((m, CONV_B_DIM), BF16),
        scratch_shapes=[pltpu.VMEM((d, tn), BF16), pltpu.VMEM((SUBLANES, tn), F32)],
        compiler_params=_params(("parallel", "arbitrary")),
        name="proj_xbc",
    )(h, w_in_t, conv_w, conv_b)


def _split3(v):
    p1 = v.astype(BF16)
    r1 = v - p1.astype(F32)
    p2 = r1.astype(BF16)
    p3 = (r1 - p2.astype(F32)).astype(BF16)
    return p1, p2, p3


def _dt_kernel(h_ref, wdt_ref, bias_ref, alog_ref, acol_ref, rowp_ref, wrow_ref, alast_ref):
    x_all = lax.dot_general(h_ref[...], wdt_ref[...].astype(BF16), (((1,), (1,)), ((), ())),
                            preferred_element_type=F32) + bias_ref[...]
    a = -jnp.exp(alog_ref[...])
    li = lax.broadcasted_iota(jnp.int32, (CHUNK, CHUNK), 0)
    si = lax.broadcasted_iota(jnp.int32, (CHUNK, CHUNK), 1)
    tril = jnp.where(li >= si, 1.0, 0.0).astype(BF16)
    for q in range(h_ref.shape[0] // CHUNK):
        x = x_all[q * CHUNK:(q + 1) * CHUNK, :]
        dt = jnp.maximum(x, 0.0) + jnp.log1p(jnp.exp(-jnp.abs(x)))
        adt = dt * a
        acs = None
        for p in _split3(adt):
            term = jnp.dot(tril, p, preferred_element_type=F32)
            acs = term if acs is None else acs + term
        acol_ref[q * CHUNK:(q + 1) * CHUNK, :] = acs * LOG2_E
        alast = acs[CHUNK - 1:CHUNK, :]
        alast_ref[q] = jnp.broadcast_to(alast, (SUBLANES, HEAD_PAD))
        rowp_ref[q] = ((acs - jnp.log(dt)) * LOG2_E).T
        wrow_ref[q] = (jnp.exp(alast - acs) * dt).T


def _dt_tables(h, w_dt, dt_bias, a_log, chunks_per_step=4):
    m, d = h.shape
    nchunks = m // CHUNK
    cps = chunks_per_step
    row = pl.BlockSpec((1, HEAD_PAD), lambda i: (0, 0))
    tab = pl.BlockSpec((cps, HEAD_PAD, CHUNK), lambda i: (i, 0, 0))
    return pl.pallas_call(
        _dt_kernel,
        grid=(nchunks // cps,),
        in_specs=[
            pl.BlockSpec((cps * CHUNK, d), lambda i: (i, 0)),
            pl.BlockSpec((HEAD_PAD, d), lambda i: (0, 0)),
            row,
            row,
        ],
        out_specs=[
            pl.BlockSpec((cps * CHUNK, HEAD_PAD), lambda i: (i, 0)),
            tab,
            tab,
            pl.BlockSpec((cps, SUBLANES, HEAD_PAD), lambda i: (i, 0, 0)),
        ],
        out_shape=[
            jax.ShapeDtypeStruct((m, HEAD_PAD), F32),
            jax.ShapeDtypeStruct((nchunks, HEAD_PAD, CHUNK), F32),
            jax.ShapeDtypeStruct((nchunks, HEAD_PAD, CHUNK), F32),
            jax.ShapeDtypeStruct((nchunks, SUBLANES, HEAD_PAD), F32),
        ],
        compiler_params=_params(("parallel",)),
        name="dt_tables",
    )(h, w_dt, dt_bias, a_log)


def _ssd_kernel(xbc_ref, acol_ref, rowp_ref, wrow_ref, alast_ref, dskip_ref, zs_ref, ng_ref,
                v_ref, st_ref, vt_ref):
    @pl.when(pl.program_id(1) == 0)
    def _():
        st_ref[...] = jnp.zeros(st_ref.shape, F32)

    li = lax.broadcasted_iota(jnp.int32, (CHUNK, CHUNK), 0)
    si = lax.broadcasted_iota(jnp.int32, (CHUNK, CHUNK), 1)
    causal = li >= si
    lo = si < HEAD_DIM
    lo_row = lax.broadcasted_iota(jnp.int32, (1, LANES), 1) < HEAD_DIM
    gw = HEADS_PER_GROUP * HEAD_DIM
    ones = jnp.ones((2 * LANES, LANES), BF16)

    for g in range(N_GROUPS):
        bg = xbc_ref[:, D_INNER + g * D_STATE:D_INNER + (g + 1) * D_STATE]
        cg = xbc_ref[:, D_INNER + BC_DIM + g * D_STATE:D_INNER + BC_DIM + (g + 1) * D_STATE]
        cb = lax.dot_general(cg, bg, (((1,), (1,)), ((), ())), preferred_element_type=F32)
        bgt_bf = bg.astype(F32).T.astype(BF16)
        cb_bf = cb.astype(BF16)
        ssq = jnp.zeros((CHUNK, LANES), F32)
        for j in range(HEADS_PER_GROUP // 2):
            c0 = g * gw + j * LANES
            s_old = st_ref[:, c0:c0 + LANES]
            y_off = jnp.dot(cg, s_old.astype(BF16), preferred_element_type=F32)
            h0 = g * HEADS_PER_GROUP + 2 * j
            decay = jnp.exp(jnp.where(lo_row, alast_ref[0, 0:1, h0:h0 + 1],
                                      alast_ref[0, 0:1, h0 + 1:h0 + 2]))
            xp = xbc_ref[:, c0:c0 + LANES].astype(F32)
            xbd = jnp.concatenate([jnp.where(lo, xp, 0.0), jnp.where(lo, 0.0, xp)],
                                  axis=0).astype(BF16)
            ms, ws, cols = [], [], []
            for h in (g * HEADS_PER_GROUP + 2 * j, g * HEADS_PER_GROUP + 2 * j + 1):
                colb = jnp.broadcast_to(acol_ref[:, h:h + 1], (CHUNK, CHUNK))
                seg = colb - rowp_ref[0, h:h + 1, :]
                lmat = jnp.exp2(jnp.where(causal, seg, -jnp.inf))
                ms.append(cb_bf * lmat.astype(BF16))
                ws.append(bgt_bf * wrow_ref[0, h:h + 1, :].astype(BF16))
                cols.append(colb)
            y_diag = jnp.dot(jnp.concatenate(ms, axis=1), xbd, preferred_element_type=F32)
            s_new = jnp.dot(jnp.concatenate(ws, axis=1), xbd, preferred_element_type=F32)
            e = jnp.exp2(jnp.where(lo, cols[0], cols[1]))
            jl = slice(j * LANES, (j + 1) * LANES)
            y = y_diag + y_off * e + dskip_ref[:, c0:c0 + LANES] * xp
            st_ref[:, c0:c0 + LANES] = s_old * decay + s_new
            v = y * zs_ref[:, c0:c0 + LANES].astype(F32)
            ssq = ssq + v * v
            vt_ref[:, jl] = v
        sq_hi = ssq.astype(BF16)
        sq_lo = (ssq - sq_hi.astype(F32)).astype(BF16)
        tot = jnp.dot(jnp.concatenate([sq_hi, sq_lo], axis=1), ones, preferred_element_type=F32)
        scale = lax.rsqrt(tot * (1.0 / gw) + EPS)
        for j in range(HEADS_PER_GROUP // 2):
            c0 = g * gw + j * LANES
            jl = slice(j * LANES, (j + 1) * LANES)
            v_ref[:, c0:c0 + LANES] = (
                vt_ref[:, jl] * scale * ng_ref[:, c0:c0 + LANES]).astype(v_ref.dtype)


def _ssd(xbc, acol, rowp, wrow, alast, dskip_exp, zg, norm_g, batch, seq):
    m = xbc.shape[0]
    nc = seq // CHUNK
    tab = pl.BlockSpec((1, HEAD_PAD, CHUNK), lambda b, c: (b * nc + c, 0, 0))
    return pl.pallas_call(
        _ssd_kernel,
        grid=(batch, nc),
        in_specs=[
            pl.BlockSpec((CHUNK, CONV_B_DIM), lambda b, c: (b * nc + c, 0)),
            pl.BlockSpec((CHUNK, HEAD_PAD), lambda b, c: (b * nc + c, 0)),
            tab,
            tab,
            pl.BlockSpec((1, SUBLANES, HEAD_PAD), lambda b, c: (b * nc + c, 0, 0)),
            pl.BlockSpec((1, D_INNER), lambda b, c: (0, 0)),
            pl.BlockSpec((CHUNK, D_INNER), lambda b, c: (b * nc + c, 0)),
            pl.BlockSpec((1, D_INNER), lambda b, c: (0, 0)),
        ],
        out_specs=pl.BlockSpec((CHUNK, D_INNER), lambda b, c: (b * nc + c, 0)),
        out_shape=jax.ShapeDtypeStruct((m, D_INNER), BF16),
        scratch_shapes=[
            pltpu.VMEM((D_STATE, D_INNER), F32),
            pltpu.VMEM((CHUNK, HEADS_PER_GROUP * HEAD_DIM), F32),
        ],
        compiler_params=_params(("parallel", "arbitrary")),
        name="ssd",
    )(xbc, acol, rowp, wrow, alast, dskip_exp, zg, norm_g)


def _merge_kernel(v_ref, u_ref, ga_ref, gb_ref, wss_ref, wco_ref, o_ref):
    y_b = jnp.dot(v_ref[...], wss_ref[...], preferred_element_type=F32)
    y_a = jnp.dot(u_ref[...], wco_ref[...], preferred_element_type=F32)
    o_ref[...] = (ga_ref[...].astype(F32) * y_a + gb_ref[...].astype(F32) * y_b).astype(o_ref.dtype)


def _merge(v, zg, u, w_ssm_out, w_conv_out, tm=1024, tn=512):
    m = v.shape[0]
    n = D_MODEL
    return pl.pallas_call(
        _merge_kernel,
        grid=(m // tm, n // tn),
        in_specs=[
            pl.BlockSpec((tm, D_INNER), lambda i, j: (i, 0)),
            pl.BlockSpec((tm, D_MODEL), lambda i, j: (i, 0)),
            pl.BlockSpec((tm, tn), lambda i, j: (i, D_INNER // tn + j)),
            pl.BlockSpec((tm, tn), lambda i, j: (i, (D_INNER + n) // tn + j)),
            pl.BlockSpec((D_INNER, tn), lambda i, j: (0, j)),
            pl.BlockSpec((D_MODEL, tn), lambda i, j: (0, j)),
        ],
        out_specs=pl.BlockSpec((tm, tn), lambda i, j: (i, j)),
        out_shape=jax.ShapeDtypeStruct((m, n), BF16),
        compiler_params=_params(("parallel", "arbitrary")),
        name="merge",
    )(v, u, zg, zg, w_ssm_out, w_conv_out)


def _wo_kernel(a_ref, w_ref, x_ref, g_ref, xo_ref, h_ref, wbf_ref):
    @pl.when(pl.program_id(0) == 0)
    def _():
        _cast_rows(w_ref, wbf_ref)

    xo_ref[...] = x_ref[...] + jnp.dot(a_ref[...], wbf_ref[...], preferred_element_type=F32)
    _rmsnorm_rows(xo_ref, g_ref, h_ref, xo_ref.shape[0])


def _wo(a, w_o, layer, x, g, tm=256):
    m, k = a.shape
    n = D_MODEL
    return pl.pallas_call(
        _wo_kernel,
        grid=(m // tm,),
        in_specs=[
            pl.BlockSpec((tm, k), lambda i: (i, 0)),
            pl.BlockSpec((None, k, n), lambda i: (layer, 0, 0), pipeline_mode=pl.Buffered(1)),
            pl.BlockSpec((tm, n), lambda i: (i, 0)),
            pl.BlockSpec((1, n), lambda i: (0, 0)),
        ],
        out_specs=[
            pl.BlockSpec((tm, n), lambda i: (i, 0)),
            pl.BlockSpec((tm, n), lambda i: (i, 0)),
        ],
        out_shape=[jax.ShapeDtypeStruct((m, n), F32), jax.ShapeDtypeStruct((m, n), BF16)],
        scratch_shapes=[pltpu.VMEM((k, n), BF16)],
        compiler_params=_params(("arbitrary",)),
        name="wo",
    )(a, w_o, x, g)


def _ffn_up_kernel(h_ref, wg_ref, wu_ref, o_ref, wbf_ref):
    @pl.when(pl.program_id(1) == 0)
    def _():
        _cast_rows(wg_ref, wbf_ref, 0)
        _cast_rows(wu_ref, wbf_ref, wg_ref.shape[1])

    tn = o_ref.shape[1]
    r = jnp.dot(h_ref[...], wbf_ref[...], preferred_element_type=F32)
    o_ref[...] = (_silu(r[:, :tn]) * r[:, tn:]).astype(o_ref.dtype)


def _ffn_up(h, w_gate, w_up, layer, tm=1024, tn=512):
    m, d = h.shape
    n = w_gate.shape[2]
    wspec = pl.BlockSpec((None, d, tn), lambda j, i: (layer, 0, j))
    return pl.pallas_call(
        _ffn_up_kernel,
        grid=(n // tn, m // tm),
        in_specs=[pl.BlockSpec((tm, d), lambda j, i: (i, 0)), wspec, wspec],
        out_specs=pl.BlockSpec((tm, tn), lambda j, i: (i, j)),
        out_shape=jax.ShapeDtypeStruct((m, n), BF16),
        scratch_shapes=[pltpu.VMEM((d, 2 * tn), BF16)],
        compiler_params=_params(("parallel", "arbitrary")),
        name="ffn_up",
    )(h, w_gate, w_up)


def _ffn_down_kernel(a_ref, w_ref, x_ref, g_ref, *refs, keep_residual):
    xo_ref, h_ref = refs if keep_residual else refs[::-1]
    xo_ref[...] = x_ref[...] + jnp.dot(a_ref[...], w_ref[...], preferred_element_type=F32)
    _rmsnorm_rows(xo_ref, g_ref, h_ref, xo_ref.shape[0])


def _ffn_down(a, w, x, g, keep_residual, tm=256):
    m, kdim = a.shape
    n = w.shape[1]
    blk = pl.BlockSpec((tm, n), lambda i: (i, 0))
    if keep_residual:
        out_specs = [blk, blk]
        out_shape = [jax.ShapeDtypeStruct((m, n), F32), jax.ShapeDtypeStruct((m, n), BF16)]
        scratch = []
    else:
        out_specs = [blk]
        out_shape = [jax.ShapeDtypeStruct((m, n), F32)]
        scratch = [pltpu.VMEM((tm, n), F32)]
    return pl.pallas_call(
        functools.partial(_ffn_down_kernel, keep_residual=keep_residual),
        grid=(m // tm,),
        in_specs=[
            pl.BlockSpec((tm, kdim), lambda i: (i, 0)),
            pl.BlockSpec((kdim, n), lambda i: (0, 0), pipeline_mode=pl.Buffered(1)),
            blk,
            pl.BlockSpec((1, n), lambda i: (0, 0)),
        ],
        out_specs=out_specs,
        out_shape=out_shape,
        scratch_shapes=scratch,
        compiler_params=_params(("parallel",)),
        name="ffn_down",
    )(a, w, x, g)


def _pad_heads(v):
    return jnp.pad(v.astype(F32), (0, HEAD_PAD - N_HEADS)).reshape(1, HEAD_PAD)


def _row(v):
    return v.astype(F32).reshape(1, -1)


def kernel(x, norm_mix_g, w_in, conv_a_w, conv_a_b, ln_a_g, ln_a_b, w_conv_out, conv_b_w, conv_b_b, dt_bias, a_log, d_skip, ssm_norm_g, w_ssm_out, w_o, norm_ffn_g, w_gate, w_up, w_down, final_g):
    batch, seq, d = x.shape
    depth = w_in.shape[0]
    xf = x.reshape(batch * seq, d)
    h = _rmsnorm(xf, _row(norm_mix_g[0]))
    w_in_t = jnp.swapaxes(w_in, 1, 2)
    for l in range(depth):
        u = _proj_glu(h, w_in_t, l)
        cwb = jnp.pad(conv_b_w[l].astype(F32), ((0, SUBLANES - K_CONV_B), (0, 0)))
        xbc = _proj_xbc(h, w_in_t, l, cwb, _row(conv_b_b[l]), seq)
        cw8 = jnp.repeat(conv_a_w[l].astype(F32), SUBLANES, axis=0)
        zg, ua = _proj_zg(h, w_in_t, l, u, cw8, _row(conv_a_b[l]), _row(ln_a_g[l]),
                          _row(ln_a_b[l]), seq)
        w_dt = jnp.pad(w_in_t[l, COL_DT:COL_DT + N_HEADS, :],
                       ((0, HEAD_PAD - N_HEADS), (0, 0)))
        acol, rowp, wrow, alast = _dt_tables(h, w_dt, _pad_heads(dt_bias[l]), _pad_heads(a_log[l]))
        dskip_exp = jnp.repeat(d_skip[l].astype(F32), HEAD_DIM).reshape(1, D_INNER)
        v = _ssd(xbc, acol, rowp, wrow, alast, dskip_exp, zg, _row(ssm_norm_g[l]), batch, seq)
        merged = _merge(v, zg, ua, w_ssm_out[l].astype(BF16), w_conv_out[l].astype(BF16))
        xf, h = _wo(merged, w_o, l, xf, _row(norm_ffn_g[l]))
        act = _ffn_up(h, w_gate, w_up, l)
        w_dn = w_down[l].astype(BF16)
        if l < depth - 1:
            xf, h = _ffn_down(act, w_dn, xf, _row(norm_mix_g[l + 1]), keep_residual=True)
        else:
            (h,) = _ffn_down(act, w_dn, xf, _row(final_g), keep_residual=False)
    return h.reshape(batch, seq, d)
```

```python
import functools

import jax
import jax.numpy as jnp
from jax import lax
from jax.experimental import pallas as pl
from jax.experimental.pallas import tpu as pltpu

F32 = jnp.float32
BF16 = jnp.bfloat16

D_MODEL = 2048
K_CONV_A = 31
D_INNER = 2 * D_MODEL
HEAD_DIM = 64
N_HEADS = D_INNER // HEAD_DIM
N_GROUPS = 8
HEADS_PER_GROUP = N_HEADS // N_GROUPS
D_STATE = 128
K_CONV_B = 4
BC_DIM = N_GROUPS * D_STATE
CONV_B_DIM = D_INNER + 2 * BC_DIM
CHUNK = 128
EPS = 1e-6
LOG2_E = 1.4426950408889634

COL_GLU = 0
COL_Z = 2 * D_MODEL
COL_XBC = COL_Z + D_INNER
COL_DT = COL_XBC + CONV_B_DIM
COL_GATES = COL_DT + N_HEADS
GATE_SKEW = N_HEADS

LANES = 128
SUBLANES = 8
HEAD_PAD = LANES
VMEM_LIMIT = 56 * 1024 * 1024


def _params(sem):
    return pltpu.CompilerParams(dimension_semantics=sem, vmem_limit_bytes=VMEM_LIMIT)


def _rows_loop(n_rows, blk, fn, unroll=1):
    def body(i, carry):
        fn(pl.multiple_of(i * blk, blk))
        return carry
    lax.fori_loop(0, n_rows // blk, body, 0, unroll=unroll)


def _cast_rows(src_ref, dst_ref, col0=0):
    w = src_ref.shape[1]
    def blk(r0):
        dst_ref[pl.ds(r0, 32), col0:col0 + w] = src_ref[pl.ds(r0, 32), :].astype(BF16)
    _rows_loop(src_ref.shape[0], 32, blk)


T_BLK = 256


def _cast_transposed(get_block, n_rows, dst_ref, col0=0):
    for n0 in range(0, n_rows, T_BLK):
        def body(kb, carry, n0=n0):
            k0 = pl.multiple_of(kb * T_BLK, T_BLK)
            dst_ref[pl.ds(k0, T_BLK), col0 + n0:col0 + n0 + T_BLK] = (
                get_block(n0, k0).T.astype(BF16))
            return carry
        lax.fori_loop(0, dst_ref.shape[0] // T_BLK, body, 0, unroll=4)


def _block_of(ref):
    return lambda n0, k0: ref[n0:n0 + T_BLK, pl.ds(k0, T_BLK)]


def _rmsnorm_rows(x_ref, g_ref, h_ref, n_rows):
    def blk(r0):
        x = x_ref[pl.ds(r0, 16), :]
        ms = jnp.mean(x * x, axis=-1, keepdims=True)
        h_ref[pl.ds(r0, 16), :] = (x * lax.rsqrt(ms + EPS) * g_ref[...]).astype(h_ref.dtype)
    _rows_loop(n_rows, 16, blk, unroll=8)


def _silu(v):
    return v * jax.nn.sigmoid(v)


def _rmsnorm_kernel(x_ref, g_ref, h_ref):
    _rmsnorm_rows(x_ref, g_ref, h_ref, x_ref.shape[0])


def _rmsnorm(x, g, tm=512):
    m, d = x.shape
    return pl.pallas_call(
        _rmsnorm_kernel,
        grid=(m // tm,),
        in_specs=[pl.BlockSpec((tm, d), lambda i: (i, 0)), pl.BlockSpec((1, d), lambda i: (0, 0))],
        out_specs=pl.BlockSpec((tm, d), lambda i: (i, 0)),
        out_shape=jax.ShapeDtypeStruct((m, d), BF16),
        compiler_params=_params(("parallel",)),
        name="rmsnorm",
    )(x, g)


def _proj_glu_kernel(h_ref, wa_ref, wb_ref, o_ref, wbf_ref):
    tn = o_ref.shape[1]

    @pl.when(pl.program_id(1) == 0)
    def _():
        _cast_transposed(_block_of(wa_ref), tn, wbf_ref, 0)
        _cast_transposed(_block_of(wb_ref), tn, wbf_ref, tn)

    r = jnp.dot(h_ref[...], wbf_ref[...], preferred_element_type=F32)
    o_ref[...] = (r[:, :tn] * jax.nn.sigmoid(r[:, tn:])).astype(o_ref.dtype)


def _proj_glu(h, w_in_t, layer, tm=1024, tn=512):
    m, d = h.shape
    nb = D_MODEL // tn
    return pl.pallas_call(
        _proj_glu_kernel,
        grid=(nb, m // tm),
        in_specs=[
            pl.BlockSpec((tm, d), lambda j, i: (i, 0)),
            pl.BlockSpec((None, tn, d), lambda j, i: (layer, COL_GLU // tn + j, 0)),
            pl.BlockSpec((None, tn, d), lambda j, i: (layer, COL_GLU // tn + nb + j, 0)),
        ],
        out_specs=pl.BlockSpec((tm, tn), lambda j, i: (i, j)),
        out_shape=jax.ShapeDtypeStruct((m, D_MODEL), BF16),
        scratch_shapes=[pltpu.VMEM((d, 2 * tn), BF16)],
        compiler_params=_params(("parallel", "arbitrary")),
        name="proj_glu",
    )(h, w_in_t, w_in_t)


HIST_A = 32
TT_A = 128
N_CHUNK = 256
ROWS_CHUNK = 32
CH_GRP = SUBLANES * LANES
N_GRP = D_MODEL // CH_GRP


def _to_time_major(x8, grp):
    tiles = [x8[:, grp * CH_GRP + j * LANES:grp * CH_GRP + (j + 1) * LANES]
             for j in range(SUBLANES)]
    return pltpu.einshape("jtl->tjl", jnp.stack(tiles, axis=0))


def _from_time_major(tiles_by_grp):
    cols = []
    for blk in tiles_by_grp:
        jt = pltpu.einshape("tjl->jtl", blk)
        cols += [jt[j] for j in range(SUBLANES)]
    return jnp.concatenate(cols, axis=1)


def _conv_a_rows(u_ref, s3_ref, cw3_ref, cb3_ref, cv3_ref, r0):
    base = HIST_A - (K_CONV_A - 1)
    for q in range(ROWS_CHUNK // 16):
        x16 = u_ref[pl.ds(r0 + q * 16, 16), :].astype(F32)
        for half in range(2):
            x8 = x16[half * SUBLANES:(half + 1) * SUBLANES, :]
            for grp in range(N_GRP):
                s3_ref[grp, pl.ds(HIST_A + r0 + q * 16 + half * SUBLANES, SUBLANES)] = (
                    _to_time_major(x8, grp))
    for grp in range(N_GRP):
        bias = cb3_ref[grp]
        for i0 in range(0, ROWS_CHUNK, SUBLANES):
            acc = [bias] * SUBLANES
            for k in range(K_CONV_A):
                w = cw3_ref[k, grp]
                for i in range(SUBLANES):
                    acc[i] = acc[i] + w * s3_ref[grp, r0 + i0 + i + base + k]
            for i in range(SUBLANES):
                cv3_ref[grp, i0 + i] = acc[i]


def _ln_silu_rows(cv3_ref, g_ref, b_ref, o_ref, r0):
    for q in range(ROWS_CHUNK // 16):
        v = jnp.concatenate(
            [_from_time_major([cv3_ref[grp, t0:t0 + SUBLANES] for grp in range(N_GRP)])
             for t0 in (q * 16, q * 16 + SUBLANES)], axis=0)
        mu = jnp.mean(v, axis=-1, keepdims=True)
        vc = v - mu
        var = jnp.mean(vc * vc, axis=-1, keepdims=True)
        y = vc * lax.rsqrt(var + EPS) * g_ref[...] + b_ref[...]
        o_ref[pl.ds(r0 + q * 16, 16), :] = _silu(y).astype(o_ref.dtype)


def _proj_zg_kernel(h_ref, wz_ref, wga_ref, wgb_ref, u_ref, cw3_ref, cb3_ref, lng_ref, lnb_ref,
                    o_ref, ua_ref, wbf_ref, s3_ref, cv3_ref, *, n_z, tiles_per_seq):
    j = pl.program_id(0)
    i = pl.program_id(1)
    d, tn = wbf_ref.shape

    @pl.when((i == 0) & (j < n_z))
    def _():
        _cast_transposed(_block_of(wz_ref), tn, wbf_ref)

    @pl.when((i == 0) & (j >= n_z))
    def _():
        def gate_block(n0, k0):
            lo, hi = n0 + GATE_SKEW, n0 + GATE_SKEW + T_BLK
            if hi <= tn:
                return wga_ref[lo:hi, pl.ds(k0, T_BLK)]
            return jnp.concatenate([wga_ref[lo:tn, pl.ds(k0, T_BLK)],
                                    wgb_ref[0:hi - tn, pl.ds(k0, T_BLK)]], axis=0)
        _cast_transposed(gate_block, tn, wbf_ref)

    t = j * pl.num_programs(1) + i

    @pl.when(t % tiles_per_seq == 0)
    def _():
        s3_ref[:, 0:HIST_A] = jnp.zeros((N_GRP, HIST_A, SUBLANES, LANES), F32)

    @pl.when(t % tiles_per_seq != 0)
    def _():
        s3_ref[:, 0:HIST_A] = s3_ref[:, TT_A:TT_A + HIST_A]

    is_z = j < n_z

    def chunk(c, carry):
        r0 = pl.multiple_of(c * ROWS_CHUNK, ROWS_CHUNK)
        _conv_a_rows(u_ref, s3_ref, cw3_ref, cb3_ref, cv3_ref, r0)
        _ln_silu_rows(cv3_ref, lng_ref, lnb_ref, ua_ref, r0)
        n0 = pl.multiple_of(c * N_CHUNK, N_CHUNK)
        r = jnp.dot(h_ref[...], wbf_ref[:, pl.ds(n0, N_CHUNK)], preferred_element_type=F32)
        sg = jax.nn.sigmoid(r)
        o_ref[:, pl.ds(n0, N_CHUNK)] = jnp.where(is_z, r * sg, sg).astype(o_ref.dtype)
        return carry
    lax.fori_loop(0, tn // N_CHUNK, chunk, 0)


def _proj_zg(h, w_in_t, layer, u, cw3, cb3, ln_g, ln_b, seq, tm=1024, tn=1024):
    m, d = h.shape
    n_z = D_INNER // tn
    n_g = 2 * D_MODEL // tn
    n_i = m // tm
    assert (n_z + n_g) * n_i * TT_A == m and tn // N_CHUNK * ROWS_CHUNK == TT_A
    gate_tile0 = (COL_GATES - GATE_SKEW) // tn
    one = pl.Buffered(1)
    row = pl.BlockSpec((1, D_MODEL), lambda j, i: (0, 0))

    def gate_tile(j):
        return gate_tile0 + jnp.maximum(j - n_z, 0)

    return pl.pallas_call(
        functools.partial(_proj_zg_kernel, n_z=n_z, tiles_per_seq=seq // TT_A),
        grid=(n_z + n_g, n_i),
        in_specs=[
            pl.BlockSpec((tm, d), lambda j, i: (i, 0)),
            pl.BlockSpec((None, tn, d),
                         lambda j, i: (layer, COL_Z // tn + jnp.minimum(j, n_z - 1), 0),
                         pipeline_mode=one),
            pl.BlockSpec((None, tn, d), lambda j, i: (layer, gate_tile(j), 0), pipeline_mode=one),
            pl.BlockSpec((None, GATE_SKEW, d),
                         lambda j, i: (layer, (gate_tile(j) + 1) * (tn // GATE_SKEW), 0),
                         pipeline_mode=one),
            pl.BlockSpec((TT_A, D_MODEL), lambda j, i: (j * n_i + i, 0)),
            pl.BlockSpec((K_CONV_A, N_GRP, SUBLANES, LANES), lambda j, i: (0, 0, 0, 0)),
            pl.BlockSpec((N_GRP, SUBLANES, LANES), lambda j, i: (0, 0, 0)),
            row, row,
        ],
        out_specs=[
            pl.BlockSpec((tm, tn), lambda j, i: (i, j)),
            pl.BlockSpec((TT_A, D_MODEL), lambda j, i: (j * n_i + i, 0)),
        ],
        out_shape=[
            jax.ShapeDtypeStruct((m, (n_z + n_g) * tn), BF16),
            jax.ShapeDtypeStruct((m, D_MODEL), BF16),
        ],
        scratch_shapes=[
            pltpu.VMEM((d, tn), BF16),
            pltpu.VMEM((N_GRP, TT_A + HIST_A, SUBLANES, LANES), F32),
            pltpu.VMEM((N_GRP, ROWS_CHUNK, SUBLANES, LANES), F32),
        ],
        compiler_params=_params(("arbitrary", "arbitrary")),
        name="proj_zg",
    )(h, w_in_t, w_in_t, w_in_t, u, cw3, cb3, ln_g, ln_b)


def _proj_xbc_kernel(h_ref, w_ref, cw_ref, cb_ref, o_ref, wbf_ref, hist_ref, *, tiles_per_seq):
    i = pl.program_id(1)

    @pl.when(i == 0)
    def _():
        _cast_transposed(_block_of(w_ref), w_ref.shape[0], wbf_ref)

    @pl.when(i % tiles_per_seq == 0)
    def _():
        hist_ref[...] = jnp.zeros(hist_ref.shape, F32)

    tm = o_ref.shape[0]
    r = jnp.dot(h_ref[...], wbf_ref[...], preferred_element_type=F32)
    ext = jnp.concatenate([hist_ref[...], r], axis=0)
    w0, w1, w2, w3 = (cw_ref[k:k + 1, :] for k in range(K_CONV_B))
    d1 = pltpu.roll(ext, 1, 0)
    d2 = pltpu.roll(ext * w1 + d1 * w0, 2, 0)
    acc = cb_ref[...] + ext * w3 + d1 * w2 + d2
    o_ref[...] = _silu(acc[SUBLANES:, :]).astype(o_ref.dtype)
    hist_ref[...] = r[tm - SUBLANES:, :]


def _proj_xbc(h, w_in_t, layer, conv_w, conv_b, seq, tm=1024, tn=1024):
    m, d = h.shape
    return pl.pallas_call(
        functools.partial(_proj_xbc_kernel, tiles_per_seq=seq // tm),
        grid=(CONV_B_DIM // tn, m // tm),
        in_specs=[
            pl.BlockSpec((tm, d), lambda j, i: (i, 0)),
            pl.BlockSpec((None, tn, d), lambda j, i: (layer, COL_XBC // tn + j, 0)),
            pl.BlockSpec((SUBLANES, tn), lambda j, i: (0, j)),
            pl.BlockSpec((1, tn), lambda j, i: (0, j)),
        ],
        out_specs=pl.BlockSpec((tm, tn), lambda j, i: (i, j)),
        out_shape=jax.ShapeDtypeStruct((m, CONV_B_DIM), BF16),
        scratch_shapes=[pltpu.VMEM((d, tn), BF16), pltpu.VMEM((SUBLANES, tn), F32)],
        compiler_params=_params(("parallel", "arbitrary")),
        name="proj_xbc",
    )(h, w_in_t, conv_w, conv_b)


def _split3(v):
    p1 = v.astype(BF16)
    r1 = v - p1.astype(F32)
    p2 = r1.astype(BF16)
    p3 = (r1 - p2.astype(F32)).astype(BF16)
    return p1, p2, p3


def _dt_kernel(h_ref, wdt_ref, bias_ref, alog_ref, acol_ref, rowp_ref, wrow_ref, alast_ref):
    x_all = lax.dot_general(h_ref[...], wdt_ref[...].astype(BF16), (((1,), (1,)), ((), ())),
                            preferred_element_type=F32) + bias_ref[...]
    a = -jnp.exp(alog_ref[...])
    li = lax.broadcasted_iota(jnp.int32, (CHUNK, CHUNK), 0)
    si = lax.broadcasted_iota(jnp.int32, (CHUNK, CHUNK), 1)
    tril = jnp.where(li >= si, 1.0, 0.0).astype(BF16)
    for q in range(h_ref.shape[0] // CHUNK):
        x = x_all[q * CHUNK:(q + 1) * CHUNK, :]
        dt = jnp.maximum(x, 0.0) + jnp.log1p(jnp.exp(-jnp.abs(x)))
        adt = dt * a
        acs = None
        for p in _split3(adt):
            term = jnp.dot(tril, p, preferred_element_type=F32)
            acs = term if acs is None else acs + term
        acol_ref[q * CHUNK:(q + 1) * CHUNK, :] = acs * LOG2_E
        alast = acs[CHUNK - 1:CHUNK, :]
        alast_ref[q] = jnp.broadcast_to(alast, (SUBLANES, HEAD_PAD))
        rowp_ref[q] = ((acs - jnp.log(dt)) * LOG2_E).T
        wrow_ref[q] = (jnp.exp(alast - acs) * dt).T


def _dt_tables(h, w_dt, dt_bias, a_log, chunks_per_step=4):
    m, d = h.shape
    nchunks = m // CHUNK
    cps = chunks_per_step
    row = pl.BlockSpec((1, HEAD_PAD), lambda i: (0, 0))
    tab = pl.BlockSpec((cps, HEAD_PAD, CHUNK), lambda i: (i, 0, 0))
    return pl.pallas_call(
        _dt_kernel,
        grid=(nchunks // cps,),
        in_specs=[
            pl.BlockSpec((cps * CHUNK, d), lambda i: (i, 0)),
            pl.BlockSpec((HEAD_PAD, d), lambda i: (0, 0)),
            row,
            row,
        ],
        out_specs=[
            pl.BlockSpec((cps * CHUNK, HEAD_PAD), lambda i: (i, 0)),
            tab,
            tab,
            pl.BlockSpec((cps, SUBLANES, HEAD_PAD), lambda i: (i, 0, 0)),
        ],
        out_shape=[
            jax.ShapeDtypeStruct((m, HEAD_PAD), F32),
            jax.ShapeDtypeStruct((nchunks, HEAD_PAD, CHUNK), F32),
            jax.ShapeDtypeStruct((nchunks, HEAD_PAD, CHUNK), F32),
            jax.ShapeDtypeStruct((nchunks, SUBLANES, HEAD_PAD), F32),
        ],
        compiler_params=_params(("parallel",)),
        name="dt_tables",
    )(h, w_dt, dt_bias, a_log)


def _ssd_kernel(xbc_ref, acol_ref, rowp_ref, wrow_ref, alast_ref, dskip_ref, zs_ref, ng_ref,
                v_ref, st_ref, vt_ref):
    @pl.when(pl.program_id(1) == 0)
    def _():
        st_ref[...] = jnp.zeros(st_ref.shape, F32)

    li = lax.broadcasted_iota(jnp.int32, (CHUNK, CHUNK), 0)
    si = lax.broadcasted_iota(jnp.int32, (CHUNK, CHUNK), 1)
    causal = li >= si
    lo = si < HEAD_DIM
    lo_row = lax.broadcasted_iota(jnp.int32, (1, LANES), 1) < HEAD_DIM
    gw = HEADS_PER_GROUP * HEAD_DIM
    ones = jnp.ones((2 * LANES, LANES), BF16)

    for g in range(N_GROUPS):
        bg = xbc_ref[:, D_INNER + g * D_STATE:D_INNER + (g + 1) * D_STATE]
        cg = xbc_ref[:, D_INNER + BC_DIM + g * D_STATE:D_INNER + BC_DIM + (g + 1) * D_STATE]
        cb = lax.dot_general(cg, bg, (((1,), (1,)), ((), ())), preferred_element_type=F32)
        bgt_bf = bg.astype(F32).T.astype(BF16)
        cb_bf = cb.astype(BF16)
        ssq = jnp.zeros((CHUNK, LANES), F32)
        for j in range(HEADS_PER_GROUP // 2):
            c0 = g * gw + j * LANES
            s_old = st_ref[:, c0:c0 + LANES]
            y_off = jnp.dot(cg, s_old.astype(BF16), preferred_element_type=F32)
            h0 = g * HEADS_PER_GROUP + 2 * j
            decay = jnp.exp(jnp.where(lo_row, alast_ref[0, 0:1, h0:h0 + 1],
                                      alast_ref[0, 0:1, h0 + 1:h0 + 2]))
            xp = xbc_ref[:, c0:c0 + LANES].astype(F32)
            xbd = jnp.concatenate([jnp.where(lo, xp, 0.0), jnp.where(lo, 0.0, xp)],
                                  axis=0).astype(BF16)
            ms, ws, cols = [], [], []
            for h in (g * HEADS_PER_GROUP + 2 * j, g * HEADS_PER_GROUP + 2 * j + 1):
                colb = jnp.broadcast_to(acol_ref[:, h:h + 1], (CHUNK, CHUNK))
                seg = colb - rowp_ref[0, h:h + 1, :]
                lmat = jnp.exp2(jnp.where(causal, seg, -jnp.inf))
                ms.append(cb_bf * lmat.astype(BF16))
                ws.append(bgt_bf * wrow_ref[0, h:h + 1, :].astype(BF16))
                cols.append(colb)
            y_diag = jnp.dot(jnp.concatenate(ms, axis=1), xbd, preferred_element_type=F32)
            s_new = jnp.dot(jnp.concatenate(ws, axis=1), xbd, preferred_element_type=F32)
            e = jnp.exp2(jnp.where(lo, cols[0], cols[1]))
            jl = slice(j * LANES, (j + 1) * LANES)
            y = y_diag + y_off * e + dskip_ref[:, c0:c0 + LANES] * xp
            st_ref[:, c0:c0 + LANES] = s_old * decay + s_new
            v = y * zs_ref[:, c0:c0 + LANES].astype(F32)
            ssq = ssq + v * v
            vt_ref[:, jl] = v
        sq_hi = ssq.astype(BF16)
        sq_lo = (ssq - sq_hi.astype(F32)).astype(BF16)
        tot = jnp.dot(jnp.concatenate([sq_hi, sq_lo], axis=1), ones, preferred_element_type=F32)
        scale = lax.rsqrt(tot * (1.0 / gw) + EPS)
        for j in range(HEADS_PER_GROUP // 2):
            c0 = g * gw + j * LANES
            jl = slice(j * LANES, (j + 1) * LANES)
            v_ref[:, c0:c0 + LANES] = (
                vt_ref[:, jl] * scale * ng_ref[:, c0:c0 + LANES]).astype(v_ref.dtype)


def _ssd(xbc, acol, rowp, wrow, alast, dskip_exp, zg, norm_g, batch, seq):
    m = xbc.shape[0]
    nc = seq // CHUNK
    tab = pl.BlockSpec((1, HEAD_PAD, CHUNK), lambda b, c: (b * nc + c, 0, 0))
    return pl.pallas_call(
        _ssd_kernel,
        grid=(batch, nc),
        in_specs=[
            pl.BlockSpec((CHUNK, CONV_B_DIM), lambda b, c: (b * nc + c, 0)),
            pl.BlockSpec((CHUNK, HEAD_PAD), lambda b, c: (b * nc + c, 0)),
            tab,
            tab,
            pl.BlockSpec((1, SUBLANES, HEAD_PAD), lambda b, c: (b * nc + c, 0, 0)),
            pl.BlockSpec((1, D_INNER), lambda b, c: (0, 0)),
            pl.BlockSpec((CHUNK, D_INNER), lambda b, c: (b * nc + c, 0)),
            pl.BlockSpec((1, D_INNER), lambda b, c: (0, 0)),
        ],
        out_specs=pl.BlockSpec((CHUNK, D_INNER), lambda b, c: (b * nc + c, 0)),
        out_shape=jax.ShapeDtypeStruct((m, D_INNER), BF16),
        scratch_shapes=[
            pltpu.VMEM((D_STATE, D_INNER), F32),
            pltpu.VMEM((CHUNK, HEADS_PER_GROUP * HEAD_DIM), F32),
        ],
        compiler_params=_params(("parallel", "arbitrary")),
        name="ssd",
    )(xbc, acol, rowp, wrow, alast, dskip_exp, zg, norm_g)


def _merge_kernel(v_ref, u_ref, ga_ref, gb_ref, wss_ref, wco_ref, o_ref):
    y_b = jnp.dot(v_ref[...], wss_ref[...], preferred_element_type=F32)
    y_a = jnp.dot(u_ref[...], wco_ref[...], preferred_element_type=F32)
    o_ref[...] = (ga_ref[...].astype(F32) * y_a + gb_ref[...].astype(F32) * y_b).astype(o_ref.dtype)


def _merge(v, zg, u, w_ssm_out, w_conv_out, tm=1024, tn=512):
    m = v.shape[0]
    n = D_MODEL
    return pl.pallas_call(
        _merge_kernel,
        grid=(m // tm, n // tn),
        in_specs=[
            pl.BlockSpec((tm, D_INNER), lambda i, j: (i, 0)),
            pl.BlockSpec((tm, D_MODEL), lambda i, j: (i, 0)),
            pl.BlockSpec((tm, tn), lambda i, j: (i, D_INNER // tn + j)),
            pl.BlockSpec((tm, tn), lambda i, j: (i, (D_INNER + n) // tn + j)),
            pl.BlockSpec((D_INNER, tn), lambda i, j: (0, j)),
            pl.BlockSpec((D_MODEL, tn), lambda i, j: (0, j)),
        ],
        out_specs=pl.BlockSpec((tm, tn), lambda i, j: (i, j)),
        out_shape=jax.ShapeDtypeStruct((m, n), BF16),
        compiler_params=_params(("parallel", "arbitrary")),
        name="merge",
    )(v, u, zg, zg, w_ssm_out, w_conv_out)


def _wo_kernel(a_ref, w_ref, x_ref, g_ref, xo_ref, h_ref, wbf_ref):
    @pl.when(pl.program_id(0) == 0)
    def _():
        _cast_rows(w_ref, wbf_ref)

    xo_ref[...] = x_ref[...] + jnp.dot(a_ref[...], wbf_ref[...], preferred_element_type=F32)
    _rmsnorm_rows(xo_ref, g_ref, h_ref, xo_ref.shape[0])


def _wo(a, w_o, layer, x, g, tm=256):
    m, k = a.shape
    n = D_MODEL
    return pl.pallas_call(
        _wo_kernel,
        grid=(m // tm,),
        in_specs=[
            pl.BlockSpec((tm, k), lambda i: (i, 0)),
            pl.BlockSpec((None, k, n), lambda i: (layer, 0, 0), pipeline_mode=pl.Buffered(1)),
            pl.BlockSpec((tm, n), lambda i: (i, 0)),
            pl.BlockSpec((1, n), lambda i: (0, 0)),
        ],
        out_specs=[
            pl.BlockSpec((tm, n), lambda i: (i, 0)),
            pl.BlockSpec((tm, n), lambda i: (i, 0)),
        ],
        out_shape=[jax.ShapeDtypeStruct((m, n), F32), jax.ShapeDtypeStruct((m, n), BF16)],
        scratch_shapes=[pltpu.VMEM((k, n), BF16)],
        compiler_params=_params(("arbitrary",)),
        name="wo",
    )(a, w_o, x, g)


def _ffn_up_kernel(h_ref, wg_ref, wu_ref, o_ref, wbf_ref):
    @pl.when(pl.program_id(1) == 0)
    def _():
        _cast_rows(wg_ref, wbf_ref, 0)
        _cast_rows(wu_ref, wbf_ref, wg_ref.shape[1])

    tn = o_ref.shape[1]
    r = jnp.dot(h_ref[...], wbf_ref[...], preferred_element_type=F32)
    o_ref[...] = (_silu(r[:, :tn]) * r[:, tn:]).astype(o_ref.dtype)


def _ffn_up(h, w_gate, w_up, layer, tm=1024, tn=512):
    m, d = h.shape
    n = w_gate.shape[2]
    wspec = pl.BlockSpec((None, d, tn), lambda j, i: (layer, 0, j))
    return pl.pallas_call(
        _ffn_up_kernel,
        grid=(n // tn, m // tm),
        in_specs=[pl.BlockSpec((tm, d), lambda j, i: (i, 0)), wspec, wspec],
        out_specs=pl.BlockSpec((tm, tn), lambda j, i: (i, j)),
        out_shape=jax.ShapeDtypeStruct((m, n), BF16),
        scratch_shapes=[pltpu.VMEM((d, 2 * tn), BF16)],
        compiler_params=_params(("parallel", "arbitrary")),
        name="ffn_up",
    )(h, w_gate, w_up)


def _ffn_down_kernel(a_ref, w_ref, x_ref, g_ref, *refs, keep_residual):
    xo_ref, h_ref = refs if keep_residual else refs[::-1]
    xo_ref[...] = x_ref[...] + jnp.dot(a_ref[...], w_ref[...], preferred_element_type=F32)
    _rmsnorm_rows(xo_ref, g_ref, h_ref, xo_ref.shape[0])


def _ffn_down(a, w, x, g, keep_residual, tm=256):
    m, kdim = a.shape
    n = w.shape[1]
    blk = pl.BlockSpec((tm, n), lambda i: (i, 0))
    if keep_residual:
        out_specs = [blk, blk]
        out_shape = [jax.ShapeDtypeStruct((m, n), F32), jax.ShapeDtypeStruct((m, n), BF16)]
        scratch = []
    else:
        out_specs = [blk]
        out_shape = [jax.ShapeDtypeStruct((m, n), F32)]
        scratch = [pltpu.VMEM((tm, n), F32)]
    return pl.pallas_call(
        functools.partial(_ffn_down_kernel, keep_residual=keep_residual),
        grid=(m // tm,),
        in_specs=[
            pl.BlockSpec((tm, kdim), lambda i: (i, 0)),
            pl.BlockSpec((kdim, n), lambda i: (0, 0), pipeline_mode=pl.Buffered(1)),
            blk,
            pl.BlockSpec((1, n), lambda i: (0, 0)),
        ],
        out_specs=out_specs,
        out_shape=out_shape,
        scratch_shapes=scratch,
        compiler_params=_params(("parallel",)),
        name="ffn_down",
    )(a, w, x, g)


def _pad_heads(v):
    return jnp.pad(v.astype(F32), (0, HEAD_PAD - N_HEADS)).reshape(1, HEAD_PAD)


def _row(v):
    return v.astype(F32).reshape(1, -1)


def kernel(x, norm_mix_g, w_in, conv_a_w, conv_a_b, ln_a_g, ln_a_b, w_conv_out, conv_b_w, conv_b_b, dt_bias, a_log, d_skip, ssm_norm_g, w_ssm_out, w_o, norm_ffn_g, w_gate, w_up, w_down, final_g):
    batch, seq, d = x.shape
    depth = w_in.shape[0]
    xf = x.reshape(batch * seq, d)
    h = _rmsnorm(xf, _row(norm_mix_g[0]))
    w_in_t = jnp.swapaxes(w_in, 1, 2)
    for l in range(depth):
        u = _proj_glu(h, w_in_t, l)
        cwb = jnp.pad(conv_b_w[l].astype(F32), ((0, SUBLANES - K_CONV_B), (0, 0)))
        xbc = _proj_xbc(h, w_in_t, l, cwb, _row(conv_b_b[l]), seq)
        cw3 = conv_a_w[l].astype(F32).reshape(K_CONV_A, N_GRP, SUBLANES, LANES)
        cb3 = conv_a_b[l].astype(F32).reshape(N_GRP, SUBLANES, LANES)
        zg, ua = _proj_zg(h, w_in_t, l, u, cw3, cb3, _row(ln_a_g[l]), _row(ln_a_b[l]), seq)
        w_dt = jnp.pad(w_in_t[l, COL_DT:COL_DT + N_HEADS, :],
                       ((0, HEAD_PAD - N_HEADS), (0, 0)))
        acol, rowp, wrow, alast = _dt_tables(h, w_dt, _pad_heads(dt_bias[l]), _pad_heads(a_log[l]))
        dskip_exp = jnp.repeat(d_skip[l].astype(F32), HEAD_DIM).reshape(1, D_INNER)
        v = _ssd(xbc, acol, rowp, wrow, alast, dskip_exp, zg, _row(ssm_norm_g[l]), batch, seq)
        merged = _merge(v, zg, ua, w_ssm_out[l].astype(BF16), w_conv_out[l].astype(BF16))
        xf, h = _wo(merged, w_o, l, xf, _row(norm_ffn_g[l]))
        act = _ffn_up(h, w_gate, w_up, l)
        w_dn = w_down[l].astype(BF16)
        if l < depth - 1:
            xf, h = _ffn_down(act, w_dn, xf, _row(norm_mix_g[l + 1]), keep_residual=True)
        else:
            (h,) = _ffn_down(act, w_dn, xf, _row(final_g), keep_residual=False)
    return h.reshape(batch, seq, d)
```

```python
import functools

import jax
import jax.numpy as jnp
from jax import lax
from jax.experimental import pallas as pl
from jax.experimental.pallas import tpu as pltpu

F32 = jnp.float32
BF16 = jnp.bfloat16

D_MODEL = 2048
K_CONV_A = 31
D_INNER = 2 * D_MODEL
HEAD_DIM = 64
N_HEADS = D_INNER // HEAD_DIM
N_GROUPS = 8
HEADS_PER_GROUP = N_HEADS // N_GROUPS
D_STATE = 128
K_CONV_B = 4
BC_DIM = N_GROUPS * D_STATE
CONV_B_DIM = D_INNER + 2 * BC_DIM
CHUNK = 128
EPS = 1e-6
LOG2_E = 1.4426950408889634

COL_GLU = 0
COL_Z = 2 * D_MODEL
COL_XBC = COL_Z + D_INNER
COL_DT = COL_XBC + CONV_B_DIM
COL_GATES = COL_DT + N_HEADS
GATE_SKEW = N_HEADS

LANES = 128
SUBLANES = 8
HEAD_PAD = LANES
VMEM_LIMIT = 56 * 1024 * 1024


def _params(sem):
    return pltpu.CompilerParams(dimension_semantics=sem, vmem_limit_bytes=VMEM_LIMIT)


def _rows_loop(n_rows, blk, fn, unroll=1):
    def body(i, carry):
        fn(pl.multiple_of(i * blk, blk))
        return carry
    lax.fori_loop(0, n_rows // blk, body, 0, unroll=unroll)


def _cast_rows(src_ref, dst_ref, col0=0):
    w = src_ref.shape[1]
    def blk(r0):
        dst_ref[pl.ds(r0, 32), col0:col0 + w] = src_ref[pl.ds(r0, 32), :].astype(BF16)
    _rows_loop(src_ref.shape[0], 32, blk)


T_BLK = 256


def _cast_transposed(get_block, n_rows, dst_ref, col0=0):
    for n0 in range(0, n_rows, T_BLK):
        def body(kb, carry, n0=n0):
            k0 = pl.multiple_of(kb * T_BLK, T_BLK)
            dst_ref[pl.ds(k0, T_BLK), col0 + n0:col0 + n0 + T_BLK] = (
                get_block(n0, k0).T.astype(BF16))
            return carry
        lax.fori_loop(0, dst_ref.shape[0] // T_BLK, body, 0, unroll=4)


def _block_of(ref):
    return lambda n0, k0: ref[n0:n0 + T_BLK, pl.ds(k0, T_BLK)]


def _rmsnorm_rows(x_ref, g_ref, h_ref, n_rows):
    def blk(r0):
        x = x_ref[pl.ds(r0, 16), :]
        ms = jnp.mean(x * x, axis=-1, keepdims=True)
        h_ref[pl.ds(r0, 16), :] = (x * lax.rsqrt(ms + EPS) * g_ref[...]).astype(h_ref.dtype)
    _rows_loop(n_rows, 16, blk, unroll=8)


def _silu(v):
    return v * jax.nn.sigmoid(v)


def _rmsnorm_kernel(x_ref, g_ref, h_ref):
    _rmsnorm_rows(x_ref, g_ref, h_ref, x_ref.shape[0])


def _rmsnorm(x, g, tm=512):
    m, d = x.shape
    return pl.pallas_call(
        _rmsnorm_kernel,
        grid=(m // tm,),
        in_specs=[pl.BlockSpec((tm, d), lambda i: (i, 0)), pl.BlockSpec((1, d), lambda i: (0, 0))],
        out_specs=pl.BlockSpec((tm, d), lambda i: (i, 0)),
        out_shape=jax.ShapeDtypeStruct((m, d), BF16),
        compiler_params=_params(("parallel",)),
        name="rmsnorm",
    )(x, g)


def _proj_glu_kernel(h_ref, wa_ref, wb_ref, o_ref, wbf_ref):
    tn = o_ref.shape[1]

    @pl.when(pl.program_id(1) == 0)
    def _():
        _cast_transposed(_block_of(wa_ref), tn, wbf_ref, 0)
        _cast_transposed(_block_of(wb_ref), tn, wbf_ref, tn)

    r = jnp.dot(h_ref[...], wbf_ref[...], preferred_element_type=F32)
    o_ref[...] = (r[:, :tn] * jax.nn.sigmoid(r[:, tn:])).astype(o_ref.dtype)


def _proj_glu(h, w_in_t, layer, tm=1024, tn=512):
    m, d = h.shape
    nb = D_MODEL // tn
    return pl.pallas_call(
        _proj_glu_kernel,
        grid=(nb, m // tm),
        in_specs=[
            pl.BlockSpec((tm, d), lambda j, i: (i, 0)),
            pl.BlockSpec((None, tn, d), lambda j, i: (layer, COL_GLU // tn + j, 0)),
            pl.BlockSpec((None, tn, d), lambda j, i: (layer, COL_GLU // tn + nb + j, 0)),
        ],
        out_specs=pl.BlockSpec((tm, tn), lambda j, i: (i, j)),
        out_shape=jax.ShapeDtypeStruct((m, D_MODEL), BF16),
        scratch_shapes=[pltpu.VMEM((d, 2 * tn), BF16)],
        compiler_params=_params(("parallel", "arbitrary")),
        name="proj_glu",
    )(h, w_in_t, w_in_t)


HIST_A = 32
TT_A = 128
N_CHUNK = 256
ROWS_CHUNK = 32
CH_GRP = SUBLANES * LANES
N_GRP = D_MODEL // CH_GRP


def _to_time_major(x8, grp):
    tiles = [x8[:, grp * CH_GRP + j * LANES:grp * CH_GRP + (j + 1) * LANES]
             for j in range(SUBLANES)]
    return pltpu.einshape("jtl->tjl", jnp.stack(tiles, axis=0))


def _from_time_major(tiles_by_grp):
    cols = []
    for blk in tiles_by_grp:
        jt = pltpu.einshape("tjl->jtl", blk)
        cols += [jt[j] for j in range(SUBLANES)]
    return jnp.concatenate(cols, axis=1)


def _conv_a_rows(u_ref, s3_ref, cw3_ref, cb3_ref, cv3_ref, r0):
    base = HIST_A - (K_CONV_A - 1)
    for q in range(ROWS_CHUNK // 16):
        x16 = u_ref[pl.ds(r0 + q * 16, 16), :].astype(F32)
        for half in range(2):
            x8 = x16[half * SUBLANES:(half + 1) * SUBLANES, :]
            for grp in range(N_GRP):
                s3_ref[grp, pl.ds(HIST_A + r0 + q * 16 + half * SUBLANES, SUBLANES)] = (
                    _to_time_major(x8, grp))
    for grp in range(N_GRP):
        bias = cb3_ref[grp]
        for i0 in range(0, ROWS_CHUNK, SUBLANES):
            acc = [bias] * SUBLANES
            for k in range(K_CONV_A):
                w = cw3_ref[k, grp]
                for i in range(SUBLANES):
                    acc[i] = acc[i] + w * s3_ref[grp, r0 + i0 + i + base + k]
            for i in range(SUBLANES):
                cv3_ref[grp, i0 + i] = acc[i]


def _ln_silu_rows(cv3_ref, g_ref, b_ref, o_ref, r0):
    for q in range(ROWS_CHUNK // 16):
        v = jnp.concatenate(
            [_from_time_major([cv3_ref[grp, t0:t0 + SUBLANES] for grp in range(N_GRP)])
             for t0 in (q * 16, q * 16 + SUBLANES)], axis=0)
        mu = jnp.mean(v, axis=-1, keepdims=True)
        vc = v - mu
        var = jnp.mean(vc * vc, axis=-1, keepdims=True)
        y = vc * lax.rsqrt(var + EPS) * g_ref[...] + b_ref[...]
        o_ref[pl.ds(r0 + q * 16, 16), :] = _silu(y).astype(o_ref.dtype)


def _proj_zg_kernel(h_ref, wz_ref, wga_ref, wgb_ref, u_ref, cw3_ref, cb3_ref, lng_ref, lnb_ref,
                    o_ref, ua_ref, wbf_ref, s3_ref, cv3_ref, *, n_z, tiles_per_seq):
    j = pl.program_id(0)
    i = pl.program_id(1)
    d, tn = wbf_ref.shape

    @pl.when((i == 0) & (j < n_z))
    def _():
        _cast_transposed(_block_of(wz_ref), tn, wbf_ref)

    @pl.when((i == 0) & (j >= n_z))
    def _():
        def gate_block(n0, k0):
            lo, hi = n0 + GATE_SKEW, n0 + GATE_SKEW + T_BLK
            if hi <= tn:
                return wga_ref[lo:hi, pl.ds(k0, T_BLK)]
            return jnp.concatenate([wga_ref[lo:tn, pl.ds(k0, T_BLK)],
                                    wgb_ref[0:hi - tn, pl.ds(k0, T_BLK)]], axis=0)
        _cast_transposed(gate_block, tn, wbf_ref)

    t = j * pl.num_programs(1) + i

    @pl.when(t % tiles_per_seq == 0)
    def _():
        s3_ref[:, 0:HIST_A] = jnp.zeros((N_GRP, HIST_A, SUBLANES, LANES), F32)

    @pl.when(t % tiles_per_seq != 0)
    def _():
        s3_ref[:, 0:HIST_A] = s3_ref[:, TT_A:TT_A + HIST_A]

    is_z = j < n_z

    def chunk(c, carry):
        r0 = pl.multiple_of(c * ROWS_CHUNK, ROWS_CHUNK)
        _conv_a_rows(u_ref, s3_ref, cw3_ref, cb3_ref, cv3_ref, r0)
        _ln_silu_rows(cv3_ref, lng_ref, lnb_ref, ua_ref, r0)
        n0 = pl.multiple_of(c * N_CHUNK, N_CHUNK)
        r = jnp.dot(h_ref[...], wbf_ref[:, pl.ds(n0, N_CHUNK)], preferred_element_type=F32)
        sg = jax.nn.sigmoid(r)
        o_ref[:, pl.ds(n0, N_CHUNK)] = jnp.where(is_z, r * sg, sg).astype(o_ref.dtype)
        return carry
    lax.fori_loop(0, tn // N_CHUNK, chunk, 0)


def _proj_zg(h, w_in_t, layer, u, cw3, cb3, ln_g, ln_b, seq, tm=1024, tn=1024):
    m, d = h.shape
    n_z = D_INNER // tn
    n_g = 2 * D_MODEL // tn
    n_i = m // tm
    assert (n_z + n_g) * n_i * TT_A == m and tn // N_CHUNK * ROWS_CHUNK == TT_A
    gate_tile0 = (COL_GATES - GATE_SKEW) // tn
    one = pl.Buffered(1)
    row = pl.BlockSpec((1, D_MODEL), lambda j, i: (0, 0))

    def gate_tile(j):
        return gate_tile0 + jnp.maximum(j - n_z, 0)

    return pl.pallas_call(
        functools.partial(_proj_zg_kernel, n_z=n_z, tiles_per_seq=seq // TT_A),
        grid=(n_z + n_g, n_i),
        in_specs=[
            pl.BlockSpec((tm, d), lambda j, i: (i, 0)),
            pl.BlockSpec((None, tn, d),
                         lambda j, i: (layer, COL_Z // tn + jnp.minimum(j, n_z - 1), 0),
                         pipeline_mode=one),
            pl.BlockSpec((None, tn, d), lambda j, i: (layer, gate_tile(j), 0), pipeline_mode=one),
            pl.BlockSpec((None, GATE_SKEW, d),
                         lambda j, i: (layer, (gate_tile(j) + 1) * (tn // GATE_SKEW), 0),
                         pipeline_mode=one),
            pl.BlockSpec((TT_A, D_MODEL), lambda j, i: (j * n_i + i, 0)),
            pl.BlockSpec((K_CONV_A, N_GRP, SUBLANES, LANES), lambda j, i: (0, 0, 0, 0)),
            pl.BlockSpec((N_GRP, SUBLANES, LANES), lambda j, i: (0, 0, 0)),
            row, row,
        ],
        out_specs=[
            pl.BlockSpec((tm, tn), lambda j, i: (i, j)),
            pl.BlockSpec((TT_A, D_MODEL), lambda j, i: (j * n_i + i, 0)),
        ],
        out_shape=[
            jax.ShapeDtypeStruct((m, (n_z + n_g) * tn), BF16),
            jax.ShapeDtypeStruct((m, D_MODEL), BF16),
        ],
        scratch_shapes=[
            pltpu.VMEM((d, tn), BF16),
            pltpu.VMEM((N_GRP, TT_A + HIST_A, SUBLANES, LANES), F32),
            pltpu.VMEM((N_GRP, ROWS_CHUNK, SUBLANES, LANES), F32),
        ],
        compiler_params=_params(("arbitrary", "arbitrary")),
        name="proj_zg",
    )(h, w_in_t, w_in_t, w_in_t, u, cw3, cb3, ln_g, ln_b)


def _proj_xbc_kernel(h_ref, w_ref, cw_ref, cb_ref, o_ref, wbf_ref, hist_ref, *, tiles_per_seq):
    i = pl.program_id(1)

    @pl.when(i == 0)
    def _():
        _cast_transposed(_block_of(w_ref), w_ref.shape[0], wbf_ref)

    @pl.when(i % tiles_per_seq == 0)
    def _():
        hist_ref[...] = jnp.zeros(hist_ref.shape, F32)

    tm = o_ref.shape[0]
    r = jnp.dot(h_ref[...], wbf_ref[...], preferred_element_type=F32)
    ext = jnp.concatenate([hist_ref[...], r], axis=0)
    w0, w1, w2, w3 = (cw_ref[k:k + 1, :] for k in range(K_CONV_B))
    d1 = pltpu.roll(ext, 1, 0)
    d2 = pltpu.roll(ext * w1 + d1 * w0, 2, 0)
    acc = cb_ref[...] + ext * w3 + d1 * w2 + d2
    o_ref[...] = _silu(acc[SUBLANES:, :]).astype(o_ref.dtype)
    hist_ref[...] = r[tm - SUBLANES:, :]


def _proj_xbc(h, w_in_t, layer, conv_w, conv_b, seq, tm=1024, tn=1024):
    m, d = h.shape
    return pl.pallas_call(
        functools.partial(_proj_xbc_kernel, tiles_per_seq=seq // tm),
        grid=(CONV_B_DIM // tn, m // tm),
        in_specs=[
            pl.BlockSpec((tm, d), lambda j, i: (i, 0)),
            pl.BlockSpec((None, tn, d), lambda j, i: (layer, COL_XBC // tn + j, 0)),
            pl.BlockSpec((SUBLANES, tn), lambda j, i: (0, j)),
            pl.BlockSpec((1, tn), lambda j, i: (0, j)),
        ],
        out_specs=pl.BlockSpec((tm, tn), lambda j, i: (i, j)),
        out_shape=jax.ShapeDtypeStruct((m, CONV_B_DIM), BF16),
        scratch_shapes=[pltpu.VMEM((d, tn), BF16), pltpu.VMEM((SUBLANES, tn), F32)],
        compiler_params=_params(("parallel", "arbitrary")),
        name="proj_xbc",
    )(h, w_in_t, conv_w, conv_b)


def _split3(v):
    p1 = v.astype(BF16)
    r1 = v - p1.astype(F32)
    p2 = r1.astype(BF16)
    p3 = (r1 - p2.astype(F32)).astype(BF16)
    return p1, p2, p3


def _dt_kernel(h_ref, wdt_ref, bias_ref, alog_ref, acol_ref, rowp_ref, wrow_ref, alast_ref):
    x_all = lax.dot_general(h_ref[...], wdt_ref[...].astype(BF16), (((1,), (1,)), ((), ())),
                            preferred_element_type=F32) + bias_ref[...]
    a = -jnp.exp(alog_ref[...])
    li = lax.broadcasted_iota(jnp.int32, (CHUNK, CHUNK), 0)
    si = lax.broadcasted_iota(jnp.int32, (CHUNK, CHUNK), 1)
    tril = jnp.where(li >= si, 1.0, 0.0).astype(BF16)
    for q in range(h_ref.shape[0] // CHUNK):
        x = x_all[q * CHUNK:(q + 1) * CHUNK, :]
        dt = jnp.maximum(x, 0.0) + jnp.log1p(jnp.exp(-jnp.abs(x)))
        adt = dt * a
        acs = None
        for p in _split3(adt):
            term = jnp.dot(tril, p, preferred_element_type=F32)
            acs = term if acs is None else acs + term
        acol_ref[q * CHUNK:(q + 1) * CHUNK, :] = acs * LOG2_E
        alast = acs[CHUNK - 1:CHUNK, :]
        alast_ref[q] = jnp.broadcast_to(alast, (SUBLANES, HEAD_PAD))
        rowp_ref[q] = ((acs - jnp.log(dt)) * LOG2_E).T
        wrow_ref[q] = (jnp.exp(alast - acs) * dt).T


def _dt_tables(h, w_dt, dt_bias, a_log, chunks_per_step=4):
    m, d = h.shape
    nchunks = m // CHUNK
    cps = chunks_per_step
    row = pl.BlockSpec((1, HEAD_PAD), lambda i: (0, 0))
    tab = pl.BlockSpec((cps, HEAD_PAD, CHUNK), lambda i: (i, 0, 0))
    return pl.pallas_call(
        _dt_kernel,
        grid=(nchunks // cps,),
        in_specs=[
            pl.BlockSpec((cps * CHUNK, d), lambda i: (i, 0)),
            pl.BlockSpec((HEAD_PAD, d), lambda i: (0, 0)),
            row,
            row,
        ],
        out_specs=[
            pl.BlockSpec((cps * CHUNK, HEAD_PAD), lambda i: (i, 0)),
            tab,
            tab,
            pl.BlockSpec((cps, SUBLANES, HEAD_PAD), lambda i: (i, 0, 0)),
        ],
        out_shape=[
            jax.ShapeDtypeStruct((m, HEAD_PAD), F32),
            jax.ShapeDtypeStruct((nchunks, HEAD_PAD, CHUNK), F32),
            jax.ShapeDtypeStruct((nchunks, HEAD_PAD, CHUNK), F32),
            jax.ShapeDtypeStruct((nchunks, SUBLANES, HEAD_PAD), F32),
        ],
        compiler_params=_params(("parallel",)),
        name="dt_tables",
    )(h, w_dt, dt_bias, a_log)


def _ssd_kernel(xbc_ref, acol_ref, rowp_ref, wrow_ref, alast_ref, dskip_ref, zs_ref, ng_ref,
                v_ref, st_ref, vt_ref):
    @pl.when(pl.program_id(1) == 0)
    def _():
        st_ref[...] = jnp.zeros(st_ref.shape, F32)

    for q in range(xbc_ref.shape[0] // CHUNK):
        rows = slice(q * CHUNK, (q + 1) * CHUNK)
        _ssd_chunk(xbc_ref.at[rows], acol_ref.at[rows], rowp_ref.at[q:q + 1], wrow_ref.at[q:q + 1],
                   alast_ref.at[q:q + 1], dskip_ref, zs_ref.at[rows], ng_ref, v_ref.at[rows],
                   st_ref, vt_ref)


def _ssd_chunk(xbc_ref, acol_ref, rowp_ref, wrow_ref, alast_ref, dskip_ref, zs_ref, ng_ref,
               v_ref, st_ref, vt_ref):
    li = lax.broadcasted_iota(jnp.int32, (CHUNK, CHUNK), 0)
    si = lax.broadcasted_iota(jnp.int32, (CHUNK, CHUNK), 1)
    causal = li >= si
    lo = si < HEAD_DIM
    lo_row = lax.broadcasted_iota(jnp.int32, (1, LANES), 1) < HEAD_DIM
    gw = HEADS_PER_GROUP * HEAD_DIM
    ones = jnp.ones((2 * LANES, LANES), BF16)

    for g in range(N_GROUPS):
        bg = xbc_ref[:, D_INNER + g * D_STATE:D_INNER + (g + 1) * D_STATE]
        cg = xbc_ref[:, D_INNER + BC_DIM + g * D_STATE:D_INNER + BC_DIM + (g + 1) * D_STATE]
        cb = lax.dot_general(cg, bg, (((1,), (1,)), ((), ())), preferred_element_type=F32)
        bgt_bf = bg.astype(F32).T.astype(BF16)
        cb_bf = cb.astype(BF16)
        ssq = jnp.zeros((CHUNK, LANES), F32)
        for j in range(HEADS_PER_GROUP // 2):
            c0 = g * gw + j * LANES
            s_old = st_ref[:, c0:c0 + LANES]
            y_off = jnp.dot(cg, s_old.astype(BF16), preferred_element_type=F32)
            h0 = g * HEADS_PER_GROUP + 2 * j
            decay = jnp.exp(jnp.where(lo_row, alast_ref[0, 0:1, h0:h0 + 1],
                                      alast_ref[0, 0:1, h0 + 1:h0 + 2]))
            xp = xbc_ref[:, c0:c0 + LANES].astype(F32)
            xbd = jnp.concatenate([jnp.where(lo, xp, 0.0), jnp.where(lo, 0.0, xp)],
                                  axis=0).astype(BF16)
            ms, ws, cols = [], [], []
            for h in (g * HEADS_PER_GROUP + 2 * j, g * HEADS_PER_GROUP + 2 * j + 1):
                colb = jnp.broadcast_to(acol_ref[:, h:h + 1], (CHUNK, CHUNK))
                seg = colb - rowp_ref[0, h:h + 1, :]
                lmat = jnp.exp2(jnp.where(causal, seg, -jnp.inf))
                ms.append(cb_bf * lmat.astype(BF16))
                ws.append(bgt_bf * wrow_ref[0, h:h + 1, :].astype(BF16))
                cols.append(colb)
            y_diag = jnp.dot(jnp.concatenate(ms, axis=1), xbd, preferred_element_type=F32)
            s_new = jnp.dot(jnp.concatenate(ws, axis=1), xbd, preferred_element_type=F32)
            e = jnp.exp2(jnp.where(lo, cols[0], cols[1]))
            jl = slice(j * LANES, (j + 1) * LANES)
            y = y_diag + y_off * e + dskip_ref[:, c0:c0 + LANES] * xp
            st_ref[:, c0:c0 + LANES] = s_old * decay + s_new
            v = y * zs_ref[:, c0:c0 + LANES].astype(F32)
            ssq = ssq + v * v
            vt_ref[:, jl] = v
        sq_hi = ssq.astype(BF16)
        sq_lo = (ssq - sq_hi.astype(F32)).astype(BF16)
        tot = jnp.dot(jnp.concatenate([sq_hi, sq_lo], axis=1), ones, preferred_element_type=F32)
        scale = lax.rsqrt(tot * (1.0 / gw) + EPS)
        for j in range(HEADS_PER_GROUP // 2):
            c0 = g * gw + j * LANES
            jl = slice(j * LANES, (j + 1) * LANES)
            v_ref[:, c0:c0 + LANES] = (
                vt_ref[:, jl] * scale * ng_ref[:, c0:c0 + LANES]).astype(v_ref.dtype)


def _ssd(xbc, acol, rowp, wrow, alast, dskip_exp, zg, norm_g, batch, seq, chunks_per_step=2):
    m = xbc.shape[0]
    cps = chunks_per_step
    nc = seq // (cps * CHUNK)
    rows = cps * CHUNK
    tab = pl.BlockSpec((cps, HEAD_PAD, CHUNK), lambda b, c: (b * nc + c, 0, 0))
    return pl.pallas_call(
        _ssd_kernel,
        grid=(batch, nc),
        in_specs=[
            pl.BlockSpec((rows, CONV_B_DIM), lambda b, c: (b * nc + c, 0)),
            pl.BlockSpec((rows, HEAD_PAD), lambda b, c: (b * nc + c, 0)),
            tab,
            tab,
            pl.BlockSpec((cps, SUBLANES, HEAD_PAD), lambda b, c: (b * nc + c, 0, 0)),
            pl.BlockSpec((1, D_INNER), lambda b, c: (0, 0)),
            pl.BlockSpec((rows, D_INNER), lambda b, c: (b * nc + c, 0)),
            pl.BlockSpec((1, D_INNER), lambda b, c: (0, 0)),
        ],
        out_specs=pl.BlockSpec((rows, D_INNER), lambda b, c: (b * nc + c, 0)),
        out_shape=jax.ShapeDtypeStruct((m, D_INNER), BF16),
        scratch_shapes=[
            pltpu.VMEM((D_STATE, D_INNER), F32),
            pltpu.VMEM((CHUNK, HEADS_PER_GROUP * HEAD_DIM), F32),
        ],
        compiler_params=_params(("parallel", "arbitrary")),
        name="ssd",
    )(xbc, acol, rowp, wrow, alast, dskip_exp, zg, norm_g)


def _merge_kernel(v_ref, u_ref, ga_ref, gb_ref, wss_ref, wco_ref, o_ref):
    y_b = jnp.dot(v_ref[...], wss_ref[...], preferred_element_type=F32)
    y_a = jnp.dot(u_ref[...], wco_ref[...], preferred_element_type=F32)
    o_ref[...] = (ga_ref[...].astype(F32) * y_a + gb_ref[...].astype(F32) * y_b).astype(o_ref.dtype)


def _merge(v, zg, u, w_ssm_out, w_conv_out, layer, tm=1024, tn=512):
    m = v.shape[0]
    n = D_MODEL
    return pl.pallas_call(
        _merge_kernel,
        grid=(m // tm, n // tn),
        in_specs=[
            pl.BlockSpec((tm, D_INNER), lambda i, j: (i, 0)),
            pl.BlockSpec((tm, D_MODEL), lambda i, j: (i, 0)),
            pl.BlockSpec((tm, tn), lambda i, j: (i, D_INNER // tn + j)),
            pl.BlockSpec((tm, tn), lambda i, j: (i, (D_INNER + n) // tn + j)),
            pl.BlockSpec((None, D_INNER, tn), lambda i, j: (layer, 0, j)),
            pl.BlockSpec((None, D_MODEL, tn), lambda i, j: (layer, 0, j)),
        ],
        out_specs=pl.BlockSpec((tm, tn), lambda i, j: (i, j)),
        out_shape=jax.ShapeDtypeStruct((m, n), BF16),
        compiler_params=_params(("parallel", "arbitrary")),
        name="merge",
    )(v, u, zg, zg, w_ssm_out, w_conv_out)


def _wo_kernel(a_ref, w_ref, x_ref, g_ref, xo_ref, h_ref, wbf_ref):
    @pl.when(pl.program_id(0) == 0)
    def _():
        _cast_rows(w_ref, wbf_ref)

    xo_ref[...] = x_ref[...] + jnp.dot(a_ref[...], wbf_ref[...], preferred_element_type=F32)
    _rmsnorm_rows(xo_ref, g_ref, h_ref, xo_ref.shape[0])


def _wo(a, w_o, layer, x, g, tm=256):
    m, k = a.shape
    n = D_MODEL
    return pl.pallas_call(
        _wo_kernel,
        grid=(m // tm,),
        in_specs=[
            pl.BlockSpec((tm, k), lambda i: (i, 0)),
            pl.BlockSpec((None, k, n), lambda i: (layer, 0, 0), pipeline_mode=pl.Buffered(1)),
            pl.BlockSpec((tm, n), lambda i: (i, 0)),
            pl.BlockSpec((1, n), lambda i: (0, 0)),
        ],
        out_specs=[
            pl.BlockSpec((tm, n), lambda i: (i, 0)),
            pl.BlockSpec((tm, n), lambda i: (i, 0)),
        ],
        out_shape=[jax.ShapeDtypeStruct((m, n), F32), jax.ShapeDtypeStruct((m, n), BF16)],
        scratch_shapes=[pltpu.VMEM((k, n), BF16)],
        compiler_params=_params(("arbitrary",)),
        name="wo",
    )(a, w_o, x, g)


def _ffn_up_kernel(h_ref, wg_ref, wu_ref, o_ref, wbf_ref):
    @pl.when(pl.program_id(1) == 0)
    def _():
        _cast_rows(wg_ref, wbf_ref, 0)
        _cast_rows(wu_ref, wbf_ref, wg_ref.shape[1])

    tn = o_ref.shape[1]
    r = jnp.dot(h_ref[...], wbf_ref[...], preferred_element_type=F32)
    o_ref[...] = (_silu(r[:, :tn]) * r[:, tn:]).astype(o_ref.dtype)


def _ffn_up(h, w_gate, w_up, layer, tm=1024, tn=512):
    m, d = h.shape
    n = w_gate.shape[2]
    wspec = pl.BlockSpec((None, d, tn), lambda j, i: (layer, 0, j))
    return pl.pallas_call(
        _ffn_up_kernel,
        grid=(n // tn, m // tm),
        in_specs=[pl.BlockSpec((tm, d), lambda j, i: (i, 0)), wspec, wspec],
        out_specs=pl.BlockSpec((tm, tn), lambda j, i: (i, j)),
        out_shape=jax.ShapeDtypeStruct((m, n), BF16),
        scratch_shapes=[pltpu.VMEM((d, 2 * tn), BF16)],
        compiler_params=_params(("parallel", "arbitrary")),
        name="ffn_up",
    )(h, w_gate, w_up)


def _ffn_down_kernel(a_ref, w_ref, x_ref, g_ref, *refs, keep_residual):
    xo_ref, h_ref = refs if keep_residual else refs[::-1]
    xo_ref[...] = x_ref[...] + jnp.dot(a_ref[...], w_ref[...], preferred_element_type=F32)
    _rmsnorm_rows(xo_ref, g_ref, h_ref, xo_ref.shape[0])


def _ffn_down(a, w, layer, x, g, keep_residual, tm=256):
    m, kdim = a.shape
    n = w.shape[2]
    blk = pl.BlockSpec((tm, n), lambda i: (i, 0))
    if keep_residual:
        out_specs = [blk, blk]
        out_shape = [jax.ShapeDtypeStruct((m, n), F32), jax.ShapeDtypeStruct((m, n), BF16)]
        scratch = []
    else:
        out_specs = [blk]
        out_shape = [jax.ShapeDtypeStruct((m, n), F32)]
        scratch = [pltpu.VMEM((tm, n), F32)]
    return pl.pallas_call(
        functools.partial(_ffn_down_kernel, keep_residual=keep_residual),
        grid=(m // tm,),
        in_specs=[
            pl.BlockSpec((tm, kdim), lambda i: (i, 0)),
            pl.BlockSpec((None, kdim, n), lambda i: (layer, 0, 0), pipeline_mode=pl.Buffered(1)),
            blk,
            pl.BlockSpec((1, n), lambda i: (0, 0)),
        ],
        out_specs=out_specs,
        out_shape=out_shape,
        scratch_shapes=scratch,
        compiler_params=_params(("parallel",)),
        name="ffn_down",
    )(a, w, x, g)


def _pad_heads(v):
    return jnp.pad(v.astype(F32), (0, HEAD_PAD - N_HEADS)).reshape(1, HEAD_PAD)


def _row(v):
    return v.astype(F32).reshape(1, -1)


def kernel(x, norm_mix_g, w_in, conv_a_w, conv_a_b, ln_a_g, ln_a_b, w_conv_out, conv_b_w, conv_b_b, dt_bias, a_log, d_skip, ssm_norm_g, w_ssm_out, w_o, norm_ffn_g, w_gate, w_up, w_down, final_g):
    batch, seq, d = x.shape
    depth = w_in.shape[0]
    xf = x.reshape(batch * seq, d)
    h = _rmsnorm(xf, _row(norm_mix_g[0]))
    w_in_t = jnp.swapaxes(w_in, 1, 2)
    w_ssm_out_bf = w_ssm_out.astype(BF16)
    w_conv_out_bf = w_conv_out.astype(BF16)
    w_down_bf = w_down.astype(BF16)
    for l in range(depth):
        u = _proj_glu(h, w_in_t, l)
        cwb = jnp.pad(conv_b_w[l].astype(F32), ((0, SUBLANES - K_CONV_B), (0, 0)))
        xbc = _proj_xbc(h, w_in_t, l, cwb, _row(conv_b_b[l]), seq)
        cw3 = conv_a_w[l].astype(F32).reshape(K_CONV_A, N_GRP, SUBLANES, LANES)
        cb3 = conv_a_b[l].astype(F32).reshape(N_GRP, SUBLANES, LANES)
        zg, ua = _proj_zg(h, w_in_t, l, u, cw3, cb3, _row(ln_a_g[l]), _row(ln_a_b[l]), seq)
        w_dt = jnp.pad(w_in_t[l, COL_DT:COL_DT + N_HEADS, :],
                       ((0, HEAD_PAD - N_HEADS), (0, 0)))
        acol, rowp, wrow, alast = _dt_tables(h, w_dt, _pad_heads(dt_bias[l]), _pad_heads(a_log[l]))
        dskip_exp = jnp.repeat(d_skip[l].astype(F32), HEAD_DIM).reshape(1, D_INNER)
        v = _ssd(xbc, acol, rowp, wrow, alast, dskip_exp, zg, _row(ssm_norm_g[l]), batch, seq)
        merged = _merge(v, zg, ua, w_ssm_out_bf, w_conv_out_bf, l)
        xf, h = _wo(merged, w_o, l, xf, _row(norm_ffn_g[l]))
        act = _ffn_up(h, w_gate, w_up, l)
        if l < depth - 1:
            xf, h = _ffn_down(act, w_down_bf, l, xf, _row(norm_mix_g[l + 1]), keep_residual=True)
        else:
            (h,) = _ffn_down(act, w_down_bf, l, xf, _row(final_g), keep_residual=False)
    return h.reshape(batch, seq, d)
```

```python
import functools

import jax
import jax.numpy as jnp
from jax import lax
from jax.experimental import pallas as pl
from jax.experimental.pallas import tpu as pltpu

F32 = jnp.float32
BF16 = jnp.bfloat16

D_MODEL = 2048
K_CONV_A = 31
D_INNER = 2 * D_MODEL
HEAD_DIM = 64
N_HEADS = D_INNER // HEAD_DIM
N_GROUPS = 8
HEADS_PER_GROUP = N_HEADS // N_GROUPS
D_STATE = 128
K_CONV_B = 4
BC_DIM = N_GROUPS * D_STATE
CONV_B_DIM = D_INNER + 2 * BC_DIM
CHUNK = 128
EPS = 1e-6
LOG2_E = 1.4426950408889634

COL_GLU = 0
COL_Z = 2 * D_MODEL
COL_XBC = COL_Z + D_INNER
COL_DT = COL_XBC + CONV_B_DIM
COL_GATES = COL_DT + N_HEADS
GATE_SKEW = N_HEADS

LANES = 128
SUBLANES = 8
HEAD_PAD = LANES
VMEM_LIMIT = 56 * 1024 * 1024


def _params(sem):
    return pltpu.CompilerParams(dimension_semantics=sem, vmem_limit_bytes=VMEM_LIMIT)


def _rows_loop(n_rows, blk, fn, unroll=1):
    def body(i, carry):
        fn(pl.multiple_of(i * blk, blk))
        return carry
    lax.fori_loop(0, n_rows // blk, body, 0, unroll=unroll)


def _cast_rows(src_ref, dst_ref, col0=0):
    w = src_ref.shape[1]
    def blk(r0):
        dst_ref[pl.ds(r0, 32), col0:col0 + w] = src_ref[pl.ds(r0, 32), :].astype(BF16)
    _rows_loop(src_ref.shape[0], 32, blk)


T_BLK = 256


def _cast_transposed(get_block, n_rows, dst_ref, col0=0):
    for n0 in range(0, n_rows, T_BLK):
        def body(kb, carry, n0=n0):
            k0 = pl.multiple_of(kb * T_BLK, T_BLK)
            dst_ref[pl.ds(k0, T_BLK), col0 + n0:col0 + n0 + T_BLK] = (
                get_block(n0, k0).T.astype(BF16))
            return carry
        lax.fori_loop(0, dst_ref.shape[0] // T_BLK, body, 0, unroll=4)


def _block_of(ref):
    return lambda n0, k0: ref[n0:n0 + T_BLK, pl.ds(k0, T_BLK)]


def _rmsnorm_rows(x_ref, g_ref, h_ref, n_rows):
    def blk(r0):
        x = x_ref[pl.ds(r0, 16), :]
        ms = jnp.mean(x * x, axis=-1, keepdims=True)
        h_ref[pl.ds(r0, 16), :] = (x * lax.rsqrt(ms + EPS) * g_ref[...]).astype(h_ref.dtype)
    _rows_loop(n_rows, 16, blk, unroll=8)


def _silu(v):
    return v * jax.nn.sigmoid(v)


def _rmsnorm_kernel(x_ref, g_ref, h_ref):
    _rmsnorm_rows(x_ref, g_ref, h_ref, x_ref.shape[0])


def _rmsnorm(x, g, tm=512):
    m, d = x.shape
    return pl.pallas_call(
        _rmsnorm_kernel,
        grid=(m // tm,),
        in_specs=[pl.BlockSpec((tm, d), lambda i: (i, 0)), pl.BlockSpec((1, d), lambda i: (0, 0))],
        out_specs=pl.BlockSpec((tm, d), lambda i: (i, 0)),
        out_shape=jax.ShapeDtypeStruct((m, d), BF16),
        compiler_params=_params(("parallel",)),
        name="rmsnorm",
    )(x, g)


def _proj_glu_kernel(h_ref, wa_ref, wb_ref, o_ref, wbf_ref):
    tn = o_ref.shape[1]

    @pl.when(pl.program_id(1) == 0)
    def _():
        _cast_transposed(_block_of(wa_ref), tn, wbf_ref, 0)
        _cast_transposed(_block_of(wb_ref), tn, wbf_ref, tn)

    r = jnp.dot(h_ref[...], wbf_ref[...], preferred_element_type=F32)
    o_ref[...] = (r[:, :tn] * jax.nn.sigmoid(r[:, tn:])).astype(o_ref.dtype)


def _proj_glu(h, w_in_t, layer, tm=1024, tn=512):
    m, d = h.shape
    nb = D_MODEL // tn
    return pl.pallas_call(
        _proj_glu_kernel,
        grid=(nb, m // tm),
        in_specs=[
            pl.BlockSpec((tm, d), lambda j, i: (i, 0)),
            pl.BlockSpec((None, tn, d), lambda j, i: (layer, COL_GLU // tn + j, 0)),
            pl.BlockSpec((None, tn, d), lambda j, i: (layer, COL_GLU // tn + nb + j, 0)),
        ],
        out_specs=pl.BlockSpec((tm, tn), lambda j, i: (i, j)),
        out_shape=jax.ShapeDtypeStruct((m, D_MODEL), BF16),
        scratch_shapes=[pltpu.VMEM((d, 2 * tn), BF16)],
        compiler_params=_params(("parallel", "arbitrary")),
        name="proj_glu",
    )(h, w_in_t, w_in_t)


HIST_A = 32
TT_A = 128
N_CHUNK = 256
ROWS_CHUNK = 32
CH_GRP = SUBLANES * LANES
N_GRP = D_MODEL // CH_GRP


def _to_time_major(x8, grp):
    tiles = [x8[:, grp * CH_GRP + j * LANES:grp * CH_GRP + (j + 1) * LANES]
             for j in range(SUBLANES)]
    return pltpu.einshape("jtl->tjl", jnp.stack(tiles, axis=0))


def _from_time_major(tiles_by_grp):
    cols = []
    for blk in tiles_by_grp:
        jt = pltpu.einshape("tjl->jtl", blk)
        cols += [jt[j] for j in range(SUBLANES)]
    return jnp.concatenate(cols, axis=1)


def _conv_a_rows(u_ref, s3_ref, cw3_ref, cb3_ref, cv3_ref, r0):
    base = HIST_A - (K_CONV_A - 1)
    for q in range(ROWS_CHUNK // 16):
        x16 = u_ref[pl.ds(r0 + q * 16, 16), :].astype(F32)
        for half in range(2):
            x8 = x16[half * SUBLANES:(half + 1) * SUBLANES, :]
            for grp in range(N_GRP):
                s3_ref[grp, pl.ds(HIST_A + r0 + q * 16 + half * SUBLANES, SUBLANES)] = (
                    _to_time_major(x8, grp))
    for grp in range(N_GRP):
        bias = cb3_ref[grp]
        for i0 in range(0, ROWS_CHUNK, SUBLANES):
            acc = [bias] * SUBLANES
            for k in range(K_CONV_A):
                w = cw3_ref[k, grp]
                for i in range(SUBLANES):
                    acc[i] = acc[i] + w * s3_ref[grp, r0 + i0 + i + base + k]
            for i in range(SUBLANES):
                cv3_ref[grp, i0 + i] = acc[i]


def _ln_silu_rows(cv3_ref, g_ref, b_ref, o_ref, r0):
    for q in range(ROWS_CHUNK // 16):
        v = jnp.concatenate(
            [_from_time_major([cv3_ref[grp, t0:t0 + SUBLANES] for grp in range(N_GRP)])
             for t0 in (q * 16, q * 16 + SUBLANES)], axis=0)
        mu = jnp.mean(v, axis=-1, keepdims=True)
        vc = v - mu
        var = jnp.mean(vc * vc, axis=-1, keepdims=True)
        y = vc * lax.rsqrt(var + EPS) * g_ref[...] + b_ref[...]
        o_ref[pl.ds(r0 + q * 16, 16), :] = _silu(y).astype(o_ref.dtype)


def _proj_zg_kernel(h_ref, wz_ref, wga_ref, wgb_ref, u_ref, cw3_ref, cb3_ref, lng_ref, lnb_ref,
                    o_ref, ua_ref, wbf_ref, s3_ref, cv3_ref, *, n_z, tiles_per_seq):
    j = pl.program_id(0)
    i = pl.program_id(1)
    d, tn = wbf_ref.shape

    @pl.when((i == 0) & (j < n_z))
    def _():
        _cast_transposed(_block_of(wz_ref), tn, wbf_ref)

    @pl.when((i == 0) & (j >= n_z))
    def _():
        def gate_block(n0, k0):
            lo, hi = n0 + GATE_SKEW, n0 + GATE_SKEW + T_BLK
            if hi <= tn:
                return wga_ref[lo:hi, pl.ds(k0, T_BLK)]
            return jnp.concatenate([wga_ref[lo:tn, pl.ds(k0, T_BLK)],
                                    wgb_ref[0:hi - tn, pl.ds(k0, T_BLK)]], axis=0)
        _cast_transposed(gate_block, tn, wbf_ref)

    t = j * pl.num_programs(1) + i

    @pl.when(t % tiles_per_seq == 0)
    def _():
        s3_ref[:, 0:HIST_A] = jnp.zeros((N_GRP, HIST_A, SUBLANES, LANES), F32)

    @pl.when(t % tiles_per_seq != 0)
    def _():
        s3_ref[:, 0:HIST_A] = s3_ref[:, TT_A:TT_A + HIST_A]

    is_z = j < n_z

    def chunk(c, carry):
        r0 = pl.multiple_of(c * ROWS_CHUNK, ROWS_CHUNK)
        _conv_a_rows(u_ref, s3_ref, cw3_ref, cb3_ref, cv3_ref, r0)
        _ln_silu_rows(cv3_ref, lng_ref, lnb_ref, ua_ref, r0)
        n0 = pl.multiple_of(c * N_CHUNK, N_CHUNK)
        r = jnp.dot(h_ref[...], wbf_ref[:, pl.ds(n0, N_CHUNK)], preferred_element_type=F32)
        sg = jax.nn.sigmoid(r)
        o_ref[:, pl.ds(n0, N_CHUNK)] = jnp.where(is_z, r * sg, sg).astype(o_ref.dtype)
        return carry
    lax.fori_loop(0, tn // N_CHUNK, chunk, 0)


def _proj_zg(h, w_in_t, layer, u, cw3, cb3, ln_g, ln_b, seq, tm=1024, tn=1024):
    m, d = h.shape
    n_z = D_INNER // tn
    n_g = 2 * D_MODEL // tn
    n_i = m // tm
    assert (n_z + n_g) * n_i * TT_A == m and tn // N_CHUNK * ROWS_CHUNK == TT_A
    gate_tile0 = (COL_GATES - GATE_SKEW) // tn
    one = pl.Buffered(1)
    row = pl.BlockSpec((1, D_MODEL), lambda j, i: (0, 0))

    def gate_tile(j):
        return gate_tile0 + jnp.maximum(j - n_z, 0)

    return pl.pallas_call(
        functools.partial(_proj_zg_kernel, n_z=n_z, tiles_per_seq=seq // TT_A),
        grid=(n_z + n_g, n_i),
        in_specs=[
            pl.BlockSpec((tm, d), lambda j, i: (i, 0)),
            pl.BlockSpec((None, tn, d),
                         lambda j, i: (layer, COL_Z // tn + jnp.minimum(j, n_z - 1), 0),
                         pipeline_mode=one),
            pl.BlockSpec((None, tn, d), lambda j, i: (layer, gate_tile(j), 0), pipeline_mode=one),
            pl.BlockSpec((None, GATE_SKEW, d),
                         lambda j, i: (layer, (gate_tile(j) + 1) * (tn // GATE_SKEW), 0),
                         pipeline_mode=one),
            pl.BlockSpec((TT_A, D_MODEL), lambda j, i: (j * n_i + i, 0)),
            pl.BlockSpec((K_CONV_A, N_GRP, SUBLANES, LANES), lambda j, i: (0, 0, 0, 0)),
            pl.BlockSpec((N_GRP, SUBLANES, LANES), lambda j, i: (0, 0, 0)),
            row, row,
        ],
        out_specs=[
            pl.BlockSpec((tm, tn), lambda j, i: (i, j)),
            pl.BlockSpec((TT_A, D_MODEL), lambda j, i: (j * n_i + i, 0)),
        ],
        out_shape=[
            jax.ShapeDtypeStruct((m, (n_z + n_g) * tn), BF16),
            jax.ShapeDtypeStruct((m, D_MODEL), BF16),
        ],
        scratch_shapes=[
            pltpu.VMEM((d, tn), BF16),
            pltpu.VMEM((N_GRP, TT_A + HIST_A, SUBLANES, LANES), F32),
            pltpu.VMEM((N_GRP, ROWS_CHUNK, SUBLANES, LANES), F32),
        ],
        compiler_params=_params(("arbitrary", "arbitrary")),
        name="proj_zg",
    )(h, w_in_t, w_in_t, w_in_t, u, cw3, cb3, ln_g, ln_b)


def _proj_xbc_kernel(h_ref, w_ref, cw_ref, cb_ref, o_ref, wbf_ref, hist_ref, *, tiles_per_seq):
    i = pl.program_id(1)

    @pl.when(i == 0)
    def _():
        _cast_transposed(_block_of(w_ref), w_ref.shape[0], wbf_ref)

    @pl.when(i % tiles_per_seq == 0)
    def _():
        hist_ref[...] = jnp.zeros(hist_ref.shape, F32)

    tm = o_ref.shape[0]
    r = jnp.dot(h_ref[...], wbf_ref[...], preferred_element_type=F32)
    ext = jnp.concatenate([hist_ref[...], r], axis=0)
    w0, w1, w2, w3 = (cw_ref[k:k + 1, :] for k in range(K_CONV_B))
    d1 = pltpu.roll(ext, 1, 0)
    d2 = pltpu.roll(ext * w1 + d1 * w0, 2, 0)
    acc = cb_ref[...] + ext * w3 + d1 * w2 + d2
    o_ref[...] = _silu(acc[SUBLANES:, :]).astype(o_ref.dtype)
    hist_ref[...] = r[tm - SUBLANES:, :]


def _proj_xbc(h, w_in_t, layer, conv_w, conv_b, seq, tm=1024, tn=1024):
    m, d = h.shape
    return pl.pallas_call(
        functools.partial(_proj_xbc_kernel, tiles_per_seq=seq // tm),
        grid=(CONV_B_DIM // tn, m // tm),
        in_specs=[
            pl.BlockSpec((tm, d), lambda j, i: (i, 0)),
            pl.BlockSpec((None, tn, d), lambda j, i: (layer, COL_XBC // tn + j, 0)),
            pl.BlockSpec((SUBLANES, tn), lambda j, i: (0, j)),
            pl.BlockSpec((1, tn), lambda j, i: (0, j)),
        ],
        out_specs=pl.BlockSpec((tm, tn), lambda j, i: (i, j)),
        out_shape=jax.ShapeDtypeStruct((m, CONV_B_DIM), BF16),
        scratch_shapes=[pltpu.VMEM((d, tn), BF16), pltpu.VMEM((SUBLANES, tn), F32)],
        compiler_params=_params(("parallel", "arbitrary")),
        name="proj_xbc",
    )(h, w_in_t, conv_w, conv_b)


def _split3(v):
    p1 = v.astype(BF16)
    r1 = v - p1.astype(F32)
    p2 = r1.astype(BF16)
    p3 = (r1 - p2.astype(F32)).astype(BF16)
    return p1, p2, p3


def _dt_kernel(h_ref, wdt_ref, bias_ref, alog_ref, acol_ref, rowp_ref, wrow_ref, alast_ref):
    x_all = lax.dot_general(h_ref[...], wdt_ref[...].astype(BF16), (((1,), (1,)), ((), ())),
                            preferred_element_type=F32) + bias_ref[...]
    a = -jnp.exp(alog_ref[...])
    li = lax.broadcasted_iota(jnp.int32, (CHUNK, CHUNK), 0)
    si = lax.broadcasted_iota(jnp.int32, (CHUNK, CHUNK), 1)
    tril = jnp.where(li >= si, 1.0, 0.0).astype(BF16)
    for q in range(h_ref.shape[0] // CHUNK):
        x = x_all[q * CHUNK:(q + 1) * CHUNK, :]
        dt = jnp.maximum(x, 0.0) + jnp.log1p(jnp.exp(-jnp.abs(x)))
        adt = dt * a
        acs = None
        for p in _split3(adt):
            term = jnp.dot(tril, p, preferred_element_type=F32)
            acs = term if acs is None else acs + term
        acol_ref[q * CHUNK:(q + 1) * CHUNK, :] = acs * LOG2_E
        alast = acs[CHUNK - 1:CHUNK, :]
        alast_ref[q] = jnp.broadcast_to(alast, (SUBLANES, HEAD_PAD))
        rowp_ref[q] = ((acs - jnp.log(dt)) * LOG2_E).T
        wrow_ref[q] = (jnp.exp(alast - acs) * dt).T


def _dt_tables(h, w_dt, dt_bias, a_log, chunks_per_step=4):
    m, d = h.shape
    nchunks = m // CHUNK
    cps = chunks_per_step
    row = pl.BlockSpec((1, HEAD_PAD), lambda i: (0, 0))
    tab = pl.BlockSpec((cps, HEAD_PAD, CHUNK), lambda i: (i, 0, 0))
    return pl.pallas_call(
        _dt_kernel,
        grid=(nchunks // cps,),
        in_specs=[
            pl.BlockSpec((cps * CHUNK, d), lambda i: (i, 0)),
            pl.BlockSpec((HEAD_PAD, d), lambda i: (0, 0)),
            row,
            row,
        ],
        out_specs=[
            pl.BlockSpec((cps * CHUNK, HEAD_PAD), lambda i: (i, 0)),
            tab,
            tab,
            pl.BlockSpec((cps, SUBLANES, HEAD_PAD), lambda i: (i, 0, 0)),
        ],
        out_shape=[
            jax.ShapeDtypeStruct((m, HEAD_PAD), F32),
            jax.ShapeDtypeStruct((nchunks, HEAD_PAD, CHUNK), F32),
            jax.ShapeDtypeStruct((nchunks, HEAD_PAD, CHUNK), F32),
            jax.ShapeDtypeStruct((nchunks, SUBLANES, HEAD_PAD), F32),
        ],
        compiler_params=_params(("parallel",)),
        name="dt_tables",
    )(h, w_dt, dt_bias, a_log)


def _ssd_kernel(xbc_ref, acol_ref, rowp_ref, wrow_ref, alast_ref, dskip_ref, zs_ref, ng_ref,
                v_ref, st_ref, vt_ref):
    @pl.when(pl.program_id(1) == 0)
    def _():
        st_ref[...] = jnp.zeros(st_ref.shape, F32)

    for q in range(xbc_ref.shape[0] // CHUNK):
        rows = slice(q * CHUNK, (q + 1) * CHUNK)
        _ssd_chunk(xbc_ref.at[rows], acol_ref.at[rows], rowp_ref.at[q:q + 1], wrow_ref.at[q:q + 1],
                   alast_ref.at[q:q + 1], dskip_ref, zs_ref.at[rows], ng_ref, v_ref.at[rows],
                   st_ref, vt_ref)


def _ssd_chunk(xbc_ref, acol_ref, rowp_ref, wrow_ref, alast_ref, dskip_ref, zs_ref, ng_ref,
               v_ref, st_ref, vt_ref):
    li = lax.broadcasted_iota(jnp.int32, (CHUNK, CHUNK), 0)
    si = lax.broadcasted_iota(jnp.int32, (CHUNK, CHUNK), 1)
    causal = li >= si
    lo = si < HEAD_DIM
    lo_row = lax.broadcasted_iota(jnp.int32, (1, LANES), 1) < HEAD_DIM
    gw = HEADS_PER_GROUP * HEAD_DIM
    ones = jnp.ones((2 * LANES, LANES), BF16)

    for g in range(N_GROUPS):
        bg = xbc_ref[:, D_INNER + g * D_STATE:D_INNER + (g + 1) * D_STATE]
        cg = xbc_ref[:, D_INNER + BC_DIM + g * D_STATE:D_INNER + BC_DIM + (g + 1) * D_STATE]
        cb = lax.dot_general(cg, bg, (((1,), (1,)), ((), ())), preferred_element_type=F32)
        bgt_bf = bg.astype(F32).T.astype(BF16)
        cb_bf = cb.astype(BF16)
        ssq = jnp.zeros((CHUNK, LANES), F32)
        for j in range(HEADS_PER_GROUP // 2):
            c0 = g * gw + j * LANES
            s_old = st_ref[:, c0:c0 + LANES]
            y_off = jnp.dot(cg, s_old.astype(BF16), preferred_element_type=F32)
            h0 = g * HEADS_PER_GROUP + 2 * j
            decay = jnp.exp(jnp.where(lo_row, alast_ref[0, 0:1, h0:h0 + 1],
                                      alast_ref[0, 0:1, h0 + 1:h0 + 2]))
            xp = xbc_ref[:, c0:c0 + LANES].astype(F32)
            xbd = jnp.concatenate([jnp.where(lo, xp, 0.0), jnp.where(lo, 0.0, xp)],
                                  axis=0).astype(BF16)
            ms, ws, cols = [], [], []
            for h in (g * HEADS_PER_GROUP + 2 * j, g * HEADS_PER_GROUP + 2 * j + 1):
                colb = jnp.broadcast_to(acol_ref[:, h:h + 1], (CHUNK, CHUNK))
                seg = colb - rowp_ref[0, h:h + 1, :]
                lmat = jnp.exp2(jnp.where(causal, seg, -jnp.inf))
                ms.append(cb_bf * lmat.astype(BF16))
                ws.append(bgt_bf * wrow_ref[0, h:h + 1, :].astype(BF16))
                cols.append(colb)
            y_diag = jnp.dot(jnp.concatenate(ms, axis=1), xbd, preferred_element_type=F32)
            s_new = jnp.dot(jnp.concatenate(ws, axis=1), xbd, preferred_element_type=F32)
            e = jnp.exp2(jnp.where(lo, cols[0], cols[1]))
            jl = slice(j * LANES, (j + 1) * LANES)
            y = y_diag + y_off * e + dskip_ref[:, c0:c0 + LANES] * xp
            st_ref[:, c0:c0 + LANES] = s_old * decay + s_new
            v = y * zs_ref[:, c0:c0 + LANES].astype(F32)
            ssq = ssq + v * v
            vt_ref[:, jl] = v
        sq_hi = ssq.astype(BF16)
        sq_lo = (ssq - sq_hi.astype(F32)).astype(BF16)
        tot = jnp.dot(jnp.concatenate([sq_hi, sq_lo], axis=1), ones, preferred_element_type=F32)
        scale = lax.rsqrt(tot * (1.0 / gw) + EPS)
        for j in range(HEADS_PER_GROUP // 2):
            c0 = g * gw + j * LANES
            jl = slice(j * LANES, (j + 1) * LANES)
            v_ref[:, c0:c0 + LANES] = (
                vt_ref[:, jl] * scale * ng_ref[:, c0:c0 + LANES]).astype(v_ref.dtype)


def _ssd(xbc, acol, rowp, wrow, alast, dskip_exp, zg, norm_g, batch, seq, chunks_per_step=4):
    m = xbc.shape[0]
    cps = chunks_per_step
    nc = seq // (cps * CHUNK)
    rows = cps * CHUNK
    tab = pl.BlockSpec((cps, HEAD_PAD, CHUNK), lambda b, c: (b * nc + c, 0, 0))
    return pl.pallas_call(
        _ssd_kernel,
        grid=(batch, nc),
        in_specs=[
            pl.BlockSpec((rows, CONV_B_DIM), lambda b, c: (b * nc + c, 0)),
            pl.BlockSpec((rows, HEAD_PAD), lambda b, c: (b * nc + c, 0)),
            tab,
            tab,
            pl.BlockSpec((cps, SUBLANES, HEAD_PAD), lambda b, c: (b * nc + c, 0, 0)),
            pl.BlockSpec((1, D_INNER), lambda b, c: (0, 0)),
            pl.BlockSpec((rows, D_INNER), lambda b, c: (b * nc + c, 0)),
            pl.BlockSpec((1, D_INNER), lambda b, c: (0, 0)),
        ],
        out_specs=pl.BlockSpec((rows, D_INNER), lambda b, c: (b * nc + c, 0)),
        out_shape=jax.ShapeDtypeStruct((m, D_INNER), BF16),
        scratch_shapes=[
            pltpu.VMEM((D_STATE, D_INNER), F32),
            pltpu.VMEM((CHUNK, HEADS_PER_GROUP * HEAD_DIM), F32),
        ],
        compiler_params=_params(("parallel", "arbitrary")),
        name="ssd",
    )(xbc, acol, rowp, wrow, alast, dskip_exp, zg, norm_g)


def _merge_kernel(v_ref, u_ref, ga_ref, gb_ref, wss_ref, wco_ref, o_ref):
    y_b = jnp.dot(v_ref[...], wss_ref[...], preferred_element_type=F32)
    y_a = jnp.dot(u_ref[...], wco_ref[...], preferred_element_type=F32)
    o_ref[...] = (ga_ref[...].astype(F32) * y_a + gb_ref[...].astype(F32) * y_b).astype(o_ref.dtype)


def _merge(v, zg, u, w_ssm_out, w_conv_out, layer, tm=1024, tn=512):
    m = v.shape[0]
    n = D_MODEL
    return pl.pallas_call(
        _merge_kernel,
        grid=(m // tm, n // tn),
        in_specs=[
            pl.BlockSpec((tm, D_INNER), lambda i, j: (i, 0)),
            pl.BlockSpec((tm, D_MODEL), lambda i, j: (i, 0)),
            pl.BlockSpec((tm, tn), lambda i, j: (i, D_INNER // tn + j)),
            pl.BlockSpec((tm, tn), lambda i, j: (i, (D_INNER + n) // tn + j)),
            pl.BlockSpec((None, D_INNER, tn), lambda i, j: (layer, 0, j)),
            pl.BlockSpec((None, D_MODEL, tn), lambda i, j: (layer, 0, j)),
        ],
        out_specs=pl.BlockSpec((tm, tn), lambda i, j: (i, j)),
        out_shape=jax.ShapeDtypeStruct((m, n), BF16),
        compiler_params=_params(("parallel", "arbitrary")),
        name="merge",
    )(v, u, zg, zg, w_ssm_out, w_conv_out)


def _wo_kernel(a_ref, w_ref, x_ref, g_ref, xo_ref, h_ref, wbf_ref):
    @pl.when(pl.program_id(0) == 0)
    def _():
        _cast_rows(w_ref, wbf_ref)

    xo_ref[...] = x_ref[...] + jnp.dot(a_ref[...], wbf_ref[...], preferred_element_type=F32)
    _rmsnorm_rows(xo_ref, g_ref, h_ref, xo_ref.shape[0])


def _wo(a, w_o, layer, x, g, tm=256):
    m, k = a.shape
    n = D_MODEL
    return pl.pallas_call(
        _wo_kernel,
        grid=(m // tm,),
        in_specs=[
            pl.BlockSpec((tm, k), lambda i: (i, 0)),
            pl.BlockSpec((None, k, n), lambda i: (layer, 0, 0), pipeline_mode=pl.Buffered(1)),
            pl.BlockSpec((tm, n), lambda i: (i, 0)),
            pl.BlockSpec((1, n), lambda i: (0, 0)),
        ],
        out_specs=[
            pl.BlockSpec((tm, n), lambda i: (i, 0)),
            pl.BlockSpec((tm, n), lambda i: (i, 0)),
        ],
        out_shape=[jax.ShapeDtypeStruct((m, n), F32), jax.ShapeDtypeStruct((m, n), BF16)],
        scratch_shapes=[pltpu.VMEM((k, n), BF16)],
        compiler_params=_params(("arbitrary",)),
        name="wo",
    )(a, w_o, x, g)


def _ffn_up_kernel(h_ref, wg_ref, wu_ref, o_ref, wbf_ref):
    @pl.when(pl.program_id(1) == 0)
    def _():
        _cast_rows(wg_ref, wbf_ref, 0)
        _cast_rows(wu_ref, wbf_ref, wg_ref.shape[1])

    tn = o_ref.shape[1]
    r = jnp.dot(h_ref[...], wbf_ref[...], preferred_element_type=F32)
    o_ref[...] = (_silu(r[:, :tn]) * r[:, tn:]).astype(o_ref.dtype)


def _ffn_up(h, w_gate, w_up, layer, tm=1024, tn=512):
    m, d = h.shape
    n = w_gate.shape[2]
    wspec = pl.BlockSpec((None, d, tn), lambda j, i: (layer, 0, j))
    return pl.pallas_call(
        _ffn_up_kernel,
        grid=(n // tn, m // tm),
        in_specs=[pl.BlockSpec((tm, d), lambda j, i: (i, 0)), wspec, wspec],
        out_specs=pl.BlockSpec((tm, tn), lambda j, i: (i, j)),
        out_shape=jax.ShapeDtypeStruct((m, n), BF16),
        scratch_shapes=[pltpu.VMEM((d, 2 * tn), BF16)],
        compiler_params=_params(("parallel", "arbitrary")),
        name="ffn_up",
    )(h, w_gate, w_up)


def _ffn_down_kernel(a_ref, w_ref, x_ref, g_ref, *refs, keep_residual):
    xo_ref, h_ref = refs if keep_residual else refs[::-1]
    xo_ref[...] = x_ref[...] + jnp.dot(a_ref[...], w_ref[...], preferred_element_type=F32)
    _rmsnorm_rows(xo_ref, g_ref, h_ref, xo_ref.shape[0])


def _ffn_down(a, w, layer, x, g, keep_residual, tm=256):
    m, kdim = a.shape
    n = w.shape[2]
    blk = pl.BlockSpec((tm, n), lambda i: (i, 0))
    if keep_residual:
        out_specs = [blk, blk]
        out_shape = [jax.ShapeDtypeStruct((m, n), F32), jax.ShapeDtypeStruct((m, n), BF16)]
        scratch = []
    else:
        out_specs = [blk]
        out_shape = [jax.ShapeDtypeStruct((m, n), F32)]
        scratch = [pltpu.VMEM((tm, n), F32)]
    return pl.pallas_call(
        functools.partial(_ffn_down_kernel, keep_residual=keep_residual),
        grid=(m // tm,),
        in_specs=[
            pl.BlockSpec((tm, kdim), lambda i: (i, 0)),
            pl.BlockSpec((None, kdim, n), lambda i: (layer, 0, 0), pipeline_mode=pl.Buffered(1)),
            blk,
            pl.BlockSpec((1, n), lambda i: (0, 0)),
        ],
        out_specs=out_specs,
        out_shape=out_shape,
        scratch_shapes=scratch,
        compiler_params=_params(("parallel",)),
        name="ffn_down",
    )(a, w, x, g)


def _pad_heads(v):
    return jnp.pad(v.astype(F32), (0, HEAD_PAD - N_HEADS)).reshape(1, HEAD_PAD)


def _row(v):
    return v.astype(F32).reshape(1, -1)


def kernel(x, norm_mix_g, w_in, conv_a_w, conv_a_b, ln_a_g, ln_a_b, w_conv_out, conv_b_w, conv_b_b, dt_bias, a_log, d_skip, ssm_norm_g, w_ssm_out, w_o, norm_ffn_g, w_gate, w_up, w_down, final_g):
    batch, seq, d = x.shape
    depth = w_in.shape[0]
    xf = x.reshape(batch * seq, d)
    h = _rmsnorm(xf, _row(norm_mix_g[0]))
    w_in_t = jnp.swapaxes(w_in, 1, 2)
    w_ssm_out_bf = w_ssm_out.astype(BF16)
    w_conv_out_bf = w_conv_out.astype(BF16)
    w_down_bf = w_down.astype(BF16)
    for l in range(depth):
        u = _proj_glu(h, w_in_t, l)
        cwb = jnp.pad(conv_b_w[l].astype(F32), ((0, SUBLANES - K_CONV_B), (0, 0)))
        xbc = _proj_xbc(h, w_in_t, l, cwb, _row(conv_b_b[l]), seq)
        cw3 = conv_a_w[l].astype(F32).reshape(K_CONV_A, N_GRP, SUBLANES, LANES)
        cb3 = conv_a_b[l].astype(F32).reshape(N_GRP, SUBLANES, LANES)
        zg, ua = _proj_zg(h, w_in_t, l, u, cw3, cb3, _row(ln_a_g[l]), _row(ln_a_b[l]), seq)
        w_dt = jnp.pad(w_in_t[l, COL_DT:COL_DT + N_HEADS, :],
                       ((0, HEAD_PAD - N_HEADS), (0, 0)))
        acol, rowp, wrow, alast = _dt_tables(h, w_dt, _pad_heads(dt_bias[l]), _pad_heads(a_log[l]))
        dskip_exp = jnp.repeat(d_skip[l].astype(F32), HEAD_DIM).reshape(1, D_INNER)
        v = _ssd(xbc, acol, rowp, wrow, alast, dskip_exp, zg, _row(ssm_norm_g[l]), batch, seq)
        merged = _merge(v, zg, ua, w_ssm_out_bf, w_conv_out_bf, l)
        xf, h = _wo(merged, w_o, l, xf, _row(norm_ffn_g[l]))
        act = _ffn_up(h, w_gate, w_up, l)
        if l < depth - 1:
            xf, h = _ffn_down(act, w_down_bf, l, xf, _row(norm_mix_g[l + 1]), keep_residual=True)
        else:
            (h,) = _ffn_down(act, w_down_bf, l, xf, _row(final_g), keep_residual=False)
    return h.reshape(batch, seq, d)
```

```python
import functools

import jax
import jax.numpy as jnp
from jax import lax
from jax.experimental import pallas as pl
from jax.experimental.pallas import tpu as pltpu

F32 = jnp.float32
BF16 = jnp.bfloat16

D_MODEL = 2048
K_CONV_A = 31
D_INNER = 2 * D_MODEL
HEAD_DIM = 64
N_HEADS = D_INNER // HEAD_DIM
N_GROUPS = 8
HEADS_PER_GROUP = N_HEADS // N_GROUPS
D_STATE = 128
K_CONV_B = 4
BC_DIM = N_GROUPS * D_STATE
CONV_B_DIM = D_INNER + 2 * BC_DIM
CHUNK = 128
EPS = 1e-6
LOG2_E = 1.4426950408889634

COL_GLU = 0
COL_Z = 2 * D_MODEL
COL_XBC = COL_Z + D_INNER
COL_DT = COL_XBC + CONV_B_DIM
COL_GATES = COL_DT + N_HEADS
GATE_SKEW = N_HEADS

LANES = 128
SUBLANES = 8
HEAD_PAD = LANES
VMEM_LIMIT = 56 * 1024 * 1024


def _params(sem):
    return pltpu.CompilerParams(dimension_semantics=sem, vmem_limit_bytes=VMEM_LIMIT)


def _rows_loop(n_rows, blk, fn, unroll=1):
    def body(i, carry):
        fn(pl.multiple_of(i * blk, blk))
        return carry
    lax.fori_loop(0, n_rows // blk, body, 0, unroll=unroll)


def _cast_rows(src_ref, dst_ref, col0=0):
    w = src_ref.shape[1]
    def blk(r0):
        dst_ref[pl.ds(r0, 32), col0:col0 + w] = src_ref[pl.ds(r0, 32), :].astype(BF16)
    _rows_loop(src_ref.shape[0], 32, blk)


T_BLK = 256


def _cast_transposed(get_block, n_rows, dst_ref, col0=0):
    for n0 in range(0, n_rows, T_BLK):
        def body(kb, carry, n0=n0):
            k0 = pl.multiple_of(kb * T_BLK, T_BLK)
            dst_ref[pl.ds(k0, T_BLK), col0 + n0:col0 + n0 + T_BLK] = (
                get_block(n0, k0).T.astype(BF16))
            return carry
        lax.fori_loop(0, dst_ref.shape[0] // T_BLK, body, 0, unroll=4)


def _block_of(ref):
    return lambda n0, k0: ref[n0:n0 + T_BLK, pl.ds(k0, T_BLK)]


def _rmsnorm_rows(x_ref, g_ref, h_ref, n_rows):
    def blk(r0):
        x = x_ref[pl.ds(r0, 16), :]
        ms = jnp.mean(x * x, axis=-1, keepdims=True)
        h_ref[pl.ds(r0, 16), :] = (x * lax.rsqrt(ms + EPS) * g_ref[...]).astype(h_ref.dtype)
    _rows_loop(n_rows, 16, blk, unroll=8)


def _silu(v):
    return v * jax.nn.sigmoid(v)


def _rmsnorm_kernel(x_ref, g_ref, h_ref):
    _rmsnorm_rows(x_ref, g_ref, h_ref, x_ref.shape[0])


def _rmsnorm(x, g, tm=512):
    m, d = x.shape
    return pl.pallas_call(
        _rmsnorm_kernel,
        grid=(m // tm,),
        in_specs=[pl.BlockSpec((tm, d), lambda i: (i, 0)), pl.BlockSpec((1, d), lambda i: (0, 0))],
        out_specs=pl.BlockSpec((tm, d), lambda i: (i, 0)),
        out_shape=jax.ShapeDtypeStruct((m, d), BF16),
        compiler_params=_params(("parallel",)),
        name="rmsnorm",
    )(x, g)


def _proj_glu_kernel(h_ref, wa_ref, wb_ref, o_ref, wbf_ref):
    tn = o_ref.shape[1]

    @pl.when(pl.program_id(1) == 0)
    def _():
        _cast_transposed(_block_of(wa_ref), tn, wbf_ref, 0)
        _cast_transposed(_block_of(wb_ref), tn, wbf_ref, tn)

    r = jnp.dot(h_ref[...], wbf_ref[...], preferred_element_type=F32)
    o_ref[...] = (r[:, :tn] * jax.nn.sigmoid(r[:, tn:])).astype(o_ref.dtype)


def _proj_glu(h, w_in_t, layer, tm=1024, tn=512):
    m, d = h.shape
    nb = D_MODEL // tn
    return pl.pallas_call(
        _proj_glu_kernel,
        grid=(nb, m // tm),
        in_specs=[
            pl.BlockSpec((tm, d), lambda j, i: (i, 0)),
            pl.BlockSpec((None, tn, d), lambda j, i: (layer, COL_GLU // tn + j, 0)),
            pl.BlockSpec((None, tn, d), lambda j, i: (layer, COL_GLU // tn + nb + j, 0)),
        ],
        out_specs=pl.BlockSpec((tm, tn), lambda j, i: (i, j)),
        out_shape=jax.ShapeDtypeStruct((m, D_MODEL), BF16),
        scratch_shapes=[pltpu.VMEM((d, 2 * tn), BF16)],
        compiler_params=_params(("parallel", "arbitrary")),
        name="proj_glu",
    )(h, w_in_t, w_in_t)


HIST_A = 32
TT_A = 128
N_CHUNK = 256
ROWS_CHUNK = 32
CH_GRP = SUBLANES * LANES
N_GRP = D_MODEL // CH_GRP


def _to_time_major(x8, grp):
    tiles = [x8[:, grp * CH_GRP + j * LANES:grp * CH_GRP + (j + 1) * LANES]
             for j in range(SUBLANES)]
    return pltpu.einshape("jtl->tjl", jnp.stack(tiles, axis=0))


def _from_time_major(tiles_by_grp):
    cols = []
    for blk in tiles_by_grp:
        jt = pltpu.einshape("tjl->jtl", blk)
        cols += [jt[j] for j in range(SUBLANES)]
    return jnp.concatenate(cols, axis=1)


def _conv_a_rows(u_ref, s3_ref, cw3_ref, cb3_ref, cv3_ref, r0):
    base = HIST_A - (K_CONV_A - 1)
    for q in range(ROWS_CHUNK // 16):
        x16 = u_ref[pl.ds(r0 + q * 16, 16), :].astype(F32)
        for half in range(2):
            x8 = x16[half * SUBLANES:(half + 1) * SUBLANES, :]
            for grp in range(N_GRP):
                s3_ref[grp, pl.ds(HIST_A + r0 + q * 16 + half * SUBLANES, SUBLANES)] = (
                    _to_time_major(x8, grp))
    for grp in range(N_GRP):
        bias = cb3_ref[grp]
        for i0 in range(0, ROWS_CHUNK, SUBLANES):
            acc = [bias] * SUBLANES
            for k in range(K_CONV_A):
                w = cw3_ref[k, grp]
                for i in range(SUBLANES):
                    acc[i] = acc[i] + w * s3_ref[grp, r0 + i0 + i + base + k]
            for i in range(SUBLANES):
                cv3_ref[grp, i0 + i] = acc[i]


def _ln_silu_rows(cv3_ref, g_ref, b_ref, o_ref, r0):
    for q in range(ROWS_CHUNK // 16):
        v = jnp.concatenate(
            [_from_time_major([cv3_ref[grp, t0:t0 + SUBLANES] for grp in range(N_GRP)])
             for t0 in (q * 16, q * 16 + SUBLANES)], axis=0)
        mu = jnp.mean(v, axis=-1, keepdims=True)
        vc = v - mu
        var = jnp.mean(vc * vc, axis=-1, keepdims=True)
        y = vc * lax.rsqrt(var + EPS) * g_ref[...] + b_ref[...]
        o_ref[pl.ds(r0 + q * 16, 16), :] = _silu(y).astype(o_ref.dtype)


def _proj_zg_kernel(h_ref, wz_ref, wga_ref, wgb_ref, u_ref, cw3_ref, cb3_ref, lng_ref, lnb_ref,
                    o_ref, ua_ref, wbf_ref, s3_ref, cv3_ref, *, n_z, tiles_per_seq):
    j = pl.program_id(0)
    i = pl.program_id(1)
    d, tn = wbf_ref.shape

    @pl.when((i == 0) & (j < n_z))
    def _():
        _cast_transposed(_block_of(wz_ref), tn, wbf_ref)

    @pl.when((i == 0) & (j >= n_z))
    def _():
        def gate_block(n0, k0):
            lo, hi = n0 + GATE_SKEW, n0 + GATE_SKEW + T_BLK
            if hi <= tn:
                return wga_ref[lo:hi, pl.ds(k0, T_BLK)]
            return jnp.concatenate([wga_ref[lo:tn, pl.ds(k0, T_BLK)],
                                    wgb_ref[0:hi - tn, pl.ds(k0, T_BLK)]], axis=0)
        _cast_transposed(gate_block, tn, wbf_ref)

    t = j * pl.num_programs(1) + i

    @pl.when(t % tiles_per_seq == 0)
    def _():
        s3_ref[:, 0:HIST_A] = jnp.zeros((N_GRP, HIST_A, SUBLANES, LANES), F32)

    @pl.when(t % tiles_per_seq != 0)
    def _():
        s3_ref[:, 0:HIST_A] = s3_ref[:, TT_A:TT_A + HIST_A]

    is_z = j < n_z

    def chunk(c, carry):
        r0 = pl.multiple_of(c * ROWS_CHUNK, ROWS_CHUNK)
        _conv_a_rows(u_ref, s3_ref, cw3_ref, cb3_ref, cv3_ref, r0)
        _ln_silu_rows(cv3_ref, lng_ref, lnb_ref, ua_ref, r0)
        n0 = pl.multiple_of(c * N_CHUNK, N_CHUNK)
        r = jnp.dot(h_ref[...], wbf_ref[:, pl.ds(n0, N_CHUNK)], preferred_element_type=F32)
        sg = jax.nn.sigmoid(r)
        o_ref[:, pl.ds(n0, N_CHUNK)] = jnp.where(is_z, r * sg, sg).astype(o_ref.dtype)
        return carry
    lax.fori_loop(0, tn // N_CHUNK, chunk, 0)


def _proj_zg(h, w_in_t, layer, u, cw3, cb3, ln_g, ln_b, seq, tm=1024, tn=1024):
    m, d = h.shape
    n_z = D_INNER // tn
    n_g = 2 * D_MODEL // tn
    n_i = m // tm
    assert (n_z + n_g) * n_i * TT_A == m and tn // N_CHUNK * ROWS_CHUNK == TT_A
    gate_tile0 = (COL_GATES - GATE_SKEW) // tn
    one = pl.Buffered(1)
    row = pl.BlockSpec((1, D_MODEL), lambda j, i: (0, 0))

    def gate_tile(j):
        return gate_tile0 + jnp.maximum(j - n_z, 0)

    return pl.pallas_call(
        functools.partial(_proj_zg_kernel, n_z=n_z, tiles_per_seq=seq // TT_A),
        grid=(n_z + n_g, n_i),
        in_specs=[
            pl.BlockSpec((tm, d), lambda j, i: (i, 0)),
            pl.BlockSpec((None, tn, d),
                         lambda j, i: (layer, COL_Z // tn + jnp.minimum(j, n_z - 1), 0),
                         pipeline_mode=one),
            pl.BlockSpec((None, tn, d), lambda j, i: (layer, gate_tile(j), 0), pipeline_mode=one),
            pl.BlockSpec((None, GATE_SKEW, d),
                         lambda j, i: (layer, (gate_tile(j) + 1) * (tn // GATE_SKEW), 0),
                         pipeline_mode=one),
            pl.BlockSpec((TT_A, D_MODEL), lambda j, i: (j * n_i + i, 0)),
            pl.BlockSpec((K_CONV_A, N_GRP, SUBLANES, LANES), lambda j, i: (0, 0, 0, 0)),
            pl.BlockSpec((N_GRP, SUBLANES, LANES), lambda j, i: (0, 0, 0)),
            row, row,
        ],
        out_specs=[
            pl.BlockSpec((tm, tn), lambda j, i: (i, j)),
            pl.BlockSpec((TT_A, D_MODEL), lambda j, i: (j * n_i + i, 0)),
        ],
        out_shape=[
            jax.ShapeDtypeStruct((m, (n_z + n_g) * tn), BF16),
            jax.ShapeDtypeStruct((m, D_MODEL), BF16),
        ],
        scratch_shapes=[
            pltpu.VMEM((d, tn), BF16),
            pltpu.VMEM((N_GRP, TT_A + HIST_A, SUBLANES, LANES), F32),
            pltpu.VMEM((N_GRP, ROWS_CHUNK, SUBLANES, LANES), F32),
        ],
        compiler_params=_params(("arbitrary", "arbitrary")),
        name="proj_zg",
    )(h, w_in_t, w_in_t, w_in_t, u, cw3, cb3, ln_g, ln_b)


def _proj_xbc_kernel(h_ref, w_ref, cw_ref, cb_ref, o_ref, wbf_ref, hist_ref, *, tiles_per_seq):
    i = pl.program_id(1)

    @pl.when(i == 0)
    def _():
        _cast_transposed(_block_of(w_ref), w_ref.shape[0], wbf_ref)

    @pl.when(i % tiles_per_seq == 0)
    def _():
        hist_ref[...] = jnp.zeros(hist_ref.shape, F32)

    tm = o_ref.shape[0]
    r = jnp.dot(h_ref[...], wbf_ref[...], preferred_element_type=F32)
    ext = jnp.concatenate([hist_ref[...], r], axis=0)
    w0, w1, w2, w3 = (cw_ref[k:k + 1, :] for k in range(K_CONV_B))
    d1 = pltpu.roll(ext, 1, 0)
    d2 = pltpu.roll(ext * w1 + d1 * w0, 2, 0)
    acc = cb_ref[...] + ext * w3 + d1 * w2 + d2
    o_ref[...] = _silu(acc[SUBLANES:, :]).astype(o_ref.dtype)
    hist_ref[...] = r[tm - SUBLANES:, :]


def _proj_xbc(h, w_in_t, layer, conv_w, conv_b, seq, tm=1024, tn=1024):
    m, d = h.shape
    return pl.pallas_call(
        functools.partial(_proj_xbc_kernel, tiles_per_seq=seq // tm),
        grid=(CONV_B_DIM // tn, m // tm),
        in_specs=[
            pl.BlockSpec((tm, d), lambda j, i: (i, 0)),
            pl.BlockSpec((None, tn, d), lambda j, i: (layer, COL_XBC // tn + j, 0)),
            pl.BlockSpec((SUBLANES, tn), lambda j, i: (0, j)),
            pl.BlockSpec((1, tn), lambda j, i: (0, j)),
        ],
        out_specs=pl.BlockSpec((tm, tn), lambda j, i: (i, j)),
        out_shape=jax.ShapeDtypeStruct((m, CONV_B_DIM), BF16),
        scratch_shapes=[pltpu.VMEM((d, tn), BF16), pltpu.VMEM((SUBLANES, tn), F32)],
        compiler_params=_params(("parallel", "arbitrary")),
        name="proj_xbc",
    )(h, w_in_t, conv_w, conv_b)


def _split3(v):
    p1 = v.astype(BF16)
    r1 = v - p1.astype(F32)
    p2 = r1.astype(BF16)
    p3 = (r1 - p2.astype(F32)).astype(BF16)
    return p1, p2, p3


def _dt_kernel(h_ref, wdt_ref, bias_ref, alog_ref, acol_ref, rowp_ref, wrow_ref, alast_ref):
    x_all = lax.dot_general(h_ref[...], wdt_ref[...].astype(BF16), (((1,), (1,)), ((), ())),
                            preferred_element_type=F32) + bias_ref[...]
    a = -jnp.exp(alog_ref[...])
    li = lax.broadcasted_iota(jnp.int32, (CHUNK, CHUNK), 0)
    si = lax.broadcasted_iota(jnp.int32, (CHUNK, CHUNK), 1)
    tril = jnp.where(li >= si, 1.0, 0.0).astype(BF16)
    for q in range(h_ref.shape[0] // CHUNK):
        x = x_all[q * CHUNK:(q + 1) * CHUNK, :]
        dt = jnp.maximum(x, 0.0) + jnp.log1p(jnp.exp(-jnp.abs(x)))
        adt = dt * a
        acs = None
        for p in _split3(adt):
            term = jnp.dot(tril, p, preferred_element_type=F32)
            acs = term if acs is None else acs + term
        acol_ref[q * CHUNK:(q + 1) * CHUNK, :] = acs * LOG2_E
        alast = acs[CHUNK - 1:CHUNK, :]
        alast_ref[q] = jnp.broadcast_to(alast, (SUBLANES, HEAD_PAD))
        rowp_ref[q] = ((acs - jnp.log(dt)) * LOG2_E).T
        wrow_ref[q] = (jnp.exp(alast - acs) * dt).T


def _dt_tables(h, w_dt, dt_bias, a_log, chunks_per_step=4):
    m, d = h.shape
    nchunks = m // CHUNK
    cps = chunks_per_step
    row = pl.BlockSpec((1, HEAD_PAD), lambda i: (0, 0))
    tab = pl.BlockSpec((cps, HEAD_PAD, CHUNK), lambda i: (i, 0, 0))
    return pl.pallas_call(
        _dt_kernel,
        grid=(nchunks // cps,),
        in_specs=[
            pl.BlockSpec((cps * CHUNK, d), lambda i: (i, 0)),
            pl.BlockSpec((HEAD_PAD, d), lambda i: (0, 0)),
            row,
            row,
        ],
        out_specs=[
            pl.BlockSpec((cps * CHUNK, HEAD_PAD), lambda i: (i, 0)),
            tab,
            tab,
            pl.BlockSpec((cps, SUBLANES, HEAD_PAD), lambda i: (i, 0, 0)),
        ],
        out_shape=[
            jax.ShapeDtypeStruct((m, HEAD_PAD), F32),
            jax.ShapeDtypeStruct((nchunks, HEAD_PAD, CHUNK), F32),
            jax.ShapeDtypeStruct((nchunks, HEAD_PAD, CHUNK), F32),
            jax.ShapeDtypeStruct((nchunks, SUBLANES, HEAD_PAD), F32),
        ],
        compiler_params=_params(("parallel",)),
        name="dt_tables",
    )(h, w_dt, dt_bias, a_log)


def _ssd_kernel(xbc_ref, acol_ref, rowp_ref, wrow_ref, alast_ref, dskip_ref, zs_ref, ng_ref,
                v_ref, st_ref, vt_ref):
    @pl.when(pl.program_id(1) == 0)
    def _():
        st_ref[...] = jnp.zeros(st_ref.shape, F32)

    for q in range(xbc_ref.shape[0] // CHUNK):
        rows = slice(q * CHUNK, (q + 1) * CHUNK)
        _ssd_chunk(xbc_ref.at[rows], acol_ref.at[rows], rowp_ref.at[q:q + 1], wrow_ref.at[q:q + 1],
                   alast_ref.at[q:q + 1], dskip_ref, zs_ref.at[rows], ng_ref, v_ref.at[rows],
                   st_ref, vt_ref)


def _ssd_chunk(xbc_ref, acol_ref, rowp_ref, wrow_ref, alast_ref, dskip_ref, zs_ref, ng_ref,
               v_ref, st_ref, vt_ref):
    li = lax.broadcasted_iota(jnp.int32, (CHUNK, CHUNK), 0)
    si = lax.broadcasted_iota(jnp.int32, (CHUNK, CHUNK), 1)
    causal = li >= si
    lo = si < HEAD_DIM
    lo_row = lax.broadcasted_iota(jnp.int32, (1, LANES), 1) < HEAD_DIM
    gw = HEADS_PER_GROUP * HEAD_DIM
    ones = jnp.ones((2 * LANES, LANES), BF16)

    for g in range(N_GROUPS):
        bg = xbc_ref[:, D_INNER + g * D_STATE:D_INNER + (g + 1) * D_STATE]
        cg = xbc_ref[:, D_INNER + BC_DIM + g * D_STATE:D_INNER + BC_DIM + (g + 1) * D_STATE]
        cb = lax.dot_general(cg, bg, (((1,), (1,)), ((), ())), preferred_element_type=F32)
        bgt_bf = bg.astype(F32).T.astype(BF16)
        cb_bf = cb.astype(BF16)
        ssq = jnp.zeros((CHUNK, LANES), F32)
        for j in range(HEADS_PER_GROUP // 2):
            c0 = g * gw + j * LANES
            s_old = st_ref[:, c0:c0 + LANES]
            y_off = jnp.dot(cg, s_old.astype(BF16), preferred_element_type=F32)
            h0 = g * HEADS_PER_GROUP + 2 * j
            decay = jnp.exp(jnp.where(lo_row, alast_ref[0, 0:1, h0:h0 + 1],
                                      alast_ref[0, 0:1, h0 + 1:h0 + 2]))
            xp = xbc_ref[:, c0:c0 + LANES].astype(F32)
            xbd = jnp.concatenate([jnp.where(lo, xp, 0.0), jnp.where(lo, 0.0, xp)],
                                  axis=0).astype(BF16)
            ms, ws, cols = [], [], []
            for h in (g * HEADS_PER_GROUP + 2 * j, g * HEADS_PER_GROUP + 2 * j + 1):
                colb = jnp.broadcast_to(acol_ref[:, h:h + 1], (CHUNK, CHUNK))
                seg = colb - rowp_ref[0, h:h + 1, :]
                lmat = jnp.exp2(jnp.where(causal, seg, -jnp.inf))
                ms.append(cb_bf * lmat.astype(BF16))
                ws.append(bgt_bf * wrow_ref[0, h:h + 1, :].astype(BF16))
                cols.append(colb)
            y_diag = jnp.dot(jnp.concatenate(ms, axis=1), xbd, preferred_element_type=F32)
            s_new = jnp.dot(jnp.concatenate(ws, axis=1), xbd, preferred_element_type=F32)
            e = jnp.exp2(jnp.where(lo, cols[0], cols[1]))
            jl = slice(j * LANES, (j + 1) * LANES)
            y = y_diag + y_off * e + dskip_ref[:, c0:c0 + LANES] * xp
            st_ref[:, c0:c0 + LANES] = s_old * decay + s_new
            v = y * zs_ref[:, c0:c0 + LANES].astype(F32)
            ssq = ssq + v * v
            vt_ref[:, jl] = v
        sq_hi = ssq.astype(BF16)
        sq_lo = (ssq - sq_hi.astype(F32)).astype(BF16)
        tot = jnp.dot(jnp.concatenate([sq_hi, sq_lo], axis=1), ones, preferred_element_type=F32)
        scale = lax.rsqrt(tot * (1.0 / gw) + EPS)
        for j in range(HEADS_PER_GROUP // 2):
            c0 = g * gw + j * LANES
            jl = slice(j * LANES, (j + 1) * LANES)
            v_ref[:, c0:c0 + LANES] = (
                vt_ref[:, jl] * scale * ng_ref[:, c0:c0 + LANES]).astype(v_ref.dtype)


def _ssd(xbc, acol, rowp, wrow, alast, dskip_exp, zg, norm_g, batch, seq, chunks_per_step=4):
    m = xbc.shape[0]
    cps = chunks_per_step
    nc = seq // (cps * CHUNK)
    rows = cps * CHUNK
    tab = pl.BlockSpec((cps, HEAD_PAD, CHUNK), lambda b, c: (b * nc + c, 0, 0))
    return pl.pallas_call(
        _ssd_kernel,
        grid=(batch, nc),
        in_specs=[
            pl.BlockSpec((rows, CONV_B_DIM), lambda b, c: (b * nc + c, 0)),
            pl.BlockSpec((rows, HEAD_PAD), lambda b, c: (b * nc + c, 0)),
            tab,
            tab,
            pl.BlockSpec((cps, SUBLANES, HEAD_PAD), lambda b, c: (b * nc + c, 0, 0)),
            pl.BlockSpec((1, D_INNER), lambda b, c: (0, 0)),
            pl.BlockSpec((rows, D_INNER), lambda b, c: (b * nc + c, 0)),
            pl.BlockSpec((1, D_INNER), lambda b, c: (0, 0)),
        ],
        out_specs=pl.BlockSpec((rows, D_INNER), lambda b, c: (b * nc + c, 0)),
        out_shape=jax.ShapeDtypeStruct((m, D_INNER), BF16),
        scratch_shapes=[
            pltpu.VMEM((D_STATE, D_INNER), F32),
            pltpu.VMEM((CHUNK, HEADS_PER_GROUP * HEAD_DIM), F32),
        ],
        compiler_params=_params(("parallel", "arbitrary")),
        name="ssd",
    )(xbc, acol, rowp, wrow, alast, dskip_exp, zg, norm_g)


def _merge_kernel(v_ref, u_ref, ga_ref, gb_ref, wss_ref, wco_ref, o_ref):
    y_b = jnp.dot(v_ref[...], wss_ref[...], preferred_element_type=F32)
    y_a = jnp.dot(u_ref[...], wco_ref[...], preferred_element_type=F32)
    o_ref[...] = (ga_ref[...].astype(F32) * y_a + gb_ref[...].astype(F32) * y_b).astype(o_ref.dtype)


def _merge(v, zg, u, w_ssm_out, w_conv_out, layer, tm=1024, tn=512):
    m = v.shape[0]
    n = D_MODEL
    return pl.pallas_call(
        _merge_kernel,
        grid=(m // tm, n // tn),
        in_specs=[
            pl.BlockSpec((tm, D_INNER), lambda i, j: (i, 0)),
            pl.BlockSpec((tm, D_MODEL), lambda i, j: (i, 0)),
            pl.BlockSpec((tm, tn), lambda i, j: (i, D_INNER // tn + j)),
            pl.BlockSpec((tm, tn), lambda i, j: (i, (D_INNER + n) // tn + j)),
            pl.BlockSpec((None, D_INNER, tn), lambda i, j: (layer, 0, j)),
            pl.BlockSpec((None, D_MODEL, tn), lambda i, j: (layer, 0, j)),
        ],
        out_specs=pl.BlockSpec((tm, tn), lambda i, j: (i, j)),
        out_shape=jax.ShapeDtypeStruct((m, n), BF16),
        compiler_params=_params(("parallel", "arbitrary")),
        name="merge",
    )(v, u, zg, zg, w_ssm_out, w_conv_out)


def _wo_kernel(a_ref, w_ref, x_ref, g_ref, xo_ref, h_ref, wbf_ref):
    @pl.when(pl.program_id(0) == 0)
    def _():
        _cast_rows(w_ref, wbf_ref)

    xo_ref[...] = x_ref[...] + jnp.dot(a_ref[...], wbf_ref[...], preferred_element_type=F32)
    _rmsnorm_rows(xo_ref, g_ref, h_ref, xo_ref.shape[0])


def _wo(a, w_o, layer, x, g, tm=512):
    m, k = a.shape
    n = D_MODEL
    return pl.pallas_call(
        _wo_kernel,
        grid=(m // tm,),
        in_specs=[
            pl.BlockSpec((tm, k), lambda i: (i, 0)),
            pl.BlockSpec((None, k, n), lambda i: (layer, 0, 0), pipeline_mode=pl.Buffered(1)),
            pl.BlockSpec((tm, n), lambda i: (i, 0)),
            pl.BlockSpec((1, n), lambda i: (0, 0)),
        ],
        out_specs=[
            pl.BlockSpec((tm, n), lambda i: (i, 0)),
            pl.BlockSpec((tm, n), lambda i: (i, 0)),
        ],
        out_shape=[jax.ShapeDtypeStruct((m, n), F32), jax.ShapeDtypeStruct((m, n), BF16)],
        scratch_shapes=[pltpu.VMEM((k, n), BF16)],
        compiler_params=_params(("arbitrary",)),
        name="wo",
    )(a, w_o, x, g)


def _ffn_up_kernel(h_ref, wg_ref, wu_ref, o_ref, wbf_ref):
    @pl.when(pl.program_id(1) == 0)
    def _():
        _cast_rows(wg_ref, wbf_ref, 0)
        _cast_rows(wu_ref, wbf_ref, wg_ref.shape[1])

    tn = o_ref.shape[1]
    r = jnp.dot(h_ref[...], wbf_ref[...], preferred_element_type=F32)
    o_ref[...] = (_silu(r[:, :tn]) * r[:, tn:]).astype(o_ref.dtype)


def _ffn_up(h, w_gate, w_up, layer, tm=1024, tn=512):
    m, d = h.shape
    n = w_gate.shape[2]
    wspec = pl.BlockSpec((None, d, tn), lambda j, i: (layer, 0, j))
    return pl.pallas_call(
        _ffn_up_kernel,
        grid=(n // tn, m // tm),
        in_specs=[pl.BlockSpec((tm, d), lambda j, i: (i, 0)), wspec, wspec],
        out_specs=pl.BlockSpec((tm, tn), lambda j, i: (i, j)),
        out_shape=jax.ShapeDtypeStruct((m, n), BF16),
        scratch_shapes=[pltpu.VMEM((d, 2 * tn), BF16)],
        compiler_params=_params(("parallel", "arbitrary")),
        name="ffn_up",
    )(h, w_gate, w_up)


def _ffn_down_kernel(a_ref, w_ref, x_ref, g_ref, *refs, keep_residual):
    xo_ref, h_ref = refs if keep_residual else refs[::-1]
    xo_ref[...] = x_ref[...] + jnp.dot(a_ref[...], w_ref[...], preferred_element_type=F32)
    _rmsnorm_rows(xo_ref, g_ref, h_ref, xo_ref.shape[0])


def _ffn_down(a, w, layer, x, g, keep_residual, tm=256):
    m, kdim = a.shape
    n = w.shape[2]
    blk = pl.BlockSpec((tm, n), lambda i: (i, 0))
    if keep_residual:
        out_specs = [blk, blk]
        out_shape = [jax.ShapeDtypeStruct((m, n), F32), jax.ShapeDtypeStruct((m, n), BF16)]
        scratch = []
    else:
        out_specs = [blk]
        out_shape = [jax.ShapeDtypeStruct((m, n), F32)]
        scratch = [pltpu.VMEM((tm, n), F32)]
    return pl.pallas_call(
        functools.partial(_ffn_down_kernel, keep_residual=keep_residual),
        grid=(m // tm,),
        in_specs=[
            pl.BlockSpec((tm, kdim), lambda i: (i, 0)),
            pl.BlockSpec((None, kdim, n), lambda i: (layer, 0, 0), pipeline_mode=pl.Buffered(1)),
            blk,
            pl.BlockSpec((1, n), lambda i: (0, 0)),
        ],
        out_specs=out_specs,
        out_shape=out_shape,
        scratch_shapes=scratch,
        compiler_params=_params(("parallel",)),
        name="ffn_down",
    )(a, w, x, g)


def _pad_heads(v):
    return jnp.pad(v.astype(F32), (0, HEAD_PAD - N_HEADS)).reshape(1, HEAD_PAD)


def _row(v):
    return v.astype(F32).reshape(1, -1)


def kernel(x, norm_mix_g, w_in, conv_a_w, conv_a_b, ln_a_g, ln_a_b, w_conv_out, conv_b_w, conv_b_b, dt_bias, a_log, d_skip, ssm_norm_g, w_ssm_out, w_o, norm_ffn_g, w_gate, w_up, w_down, final_g):
    batch, seq, d = x.shape
    depth = w_in.shape[0]
    xf = x.reshape(batch * seq, d)
    h = _rmsnorm(xf, _row(norm_mix_g[0]))
    w_in_t = jnp.swapaxes(w_in, 1, 2)
    w_ssm_out_bf = w_ssm_out.astype(BF16)
    w_conv_out_bf = w_conv_out.astype(BF16)
    w_down_bf = w_down.astype(BF16)
    for l in range(depth):
        u = _proj_glu(h, w_in_t, l)
        cwb = jnp.pad(conv_b_w[l].astype(F32), ((0, SUBLANES - K_CONV_B), (0, 0)))
        xbc = _proj_xbc(h, w_in_t, l, cwb, _row(conv_b_b[l]), seq)
        cw3 = conv_a_w[l].astype(F32).reshape(K_CONV_A, N_GRP, SUBLANES, LANES)
        cb3 = conv_a_b[l].astype(F32).reshape(N_GRP, SUBLANES, LANES)
        zg, ua = _proj_zg(h, w_in_t, l, u, cw3, cb3, _row(ln_a_g[l]), _row(ln_a_b[l]), seq)
        w_dt = jnp.pad(w_in_t[l, COL_DT:COL_DT + N_HEADS, :],
                       ((0, HEAD_PAD - N_HEADS), (0, 0)))
        acol, rowp, wrow, alast = _dt_tables(h, w_dt, _pad_heads(dt_bias[l]), _pad_heads(a_log[l]))
        dskip_exp = jnp.repeat(d_skip[l].astype(F32), HEAD_DIM).reshape(1, D_INNER)
        v = _ssd(xbc, acol, rowp, wrow, alast, dskip_exp, zg, _row(ssm_norm_g[l]), batch, seq)
        merged = _merge(v, zg, ua, w_ssm_out_bf, w_conv_out_bf, l)
        xf, h = _wo(merged, w_o, l, xf, _row(norm_ffn_g[l]))
        act = _ffn_up(h, w_gate, w_up, l)
        if l < depth - 1:
            xf, h = _ffn_down(act, w_down_bf, l, xf, _row(norm_mix_g[l + 1]), keep_residual=True)
        else:
            (h,) = _ffn_down(act, w_down_bf, l, xf, _row(final_g), keep_residual=False)
    return h.reshape(batch, seq, d)
```
